```python
import jax, jax.numpy as jnp
from jax import lax
import numpy as np

D_MODEL = 1024
BATCH = 8
SEQ = 2048
DEPTH = 1
DEC_BATCH = 128
DEC_SEQ = 1
PAST_LEN = 16384
PAGE_SIZE = 128

D_MIX = D_MODEL
HA = 4
DHA = D_MIX // 2 // HA
W_A = HA * DHA
HB = 4
DKB = D_MIX // 2 // HB
DVB = D_MIX // 2 // HB
W_B = HB * DVB
SPLITS = tuple(int(s) for s in np.cumsum([W_A, W_A, W_A, W_A, HA, HA, HB * DKB, HB * DKB, W_B]))
PROJ_COLS = 4 * W_A + 2 * HA + 2 * HB * DKB + 2 * W_B
N_EXPERTS = 32
TOP_K = 4
D_FF = D_MODEL
SWIGLU_LIMIT = 7.0
SWIGLU_ALPHA = 1.702
MOE_BLOCK = 128
MLSTM_CHUNK = 64
HGRN_CHUNK = 32
EPS = 1e-6
NEG = -1e30

kernel_name = "hymba_mlstm_hgrn2_moe_adaln_step"


def rmsnorm(x, g):
    xf = x.astype(jnp.float32)
    y = xf * lax.rsqrt(jnp.mean(xf * xf, axis=-1, keepdims=True) + EPS) * g.astype(jnp.float32)
    return y.astype(x.dtype)


def _pad_time(t, Tp, value):
    pad = [(0, 0)] * t.ndim
    pad[2] = (0, Tp - t.shape[2])
    return jnp.pad(t, pad, constant_values=value)


def _to_chunks(t, L):
    s = t.shape
    return jnp.moveaxis(t.reshape(s[:2] + (s[2] // L, L) + s[3:]), 2, 0)


def _from_chunks(t):
    t = jnp.moveaxis(t, 0, 2)
    s = t.shape
    return t.reshape(s[:2] + (s[2] * s[3],) + s[4:])


def mlstm_chunked(q, k, v, log_i, log_f, C0, n0, m0):
    T = q.shape[2]
    L = min(MLSTM_CHUNK, T)
    Tp = -(-T // L) * L
    q, k, v = (_pad_time(t, Tp, 0.0) for t in (q, k, v))
    log_i = _pad_time(log_i, Tp, NEG)
    log_f = _pad_time(log_f, Tp, 0.0)
    causal = jnp.tril(jnp.ones((L, L), dtype=bool))

    def step(carry, xs):
        C, n, m = carry
        qc, kc, vc, lic, lfc = xs
        b = jnp.cumsum(lfc, axis=-1)
        Dm = jnp.where(causal, b[..., :, None] - b[..., None, :] + lic[..., None, :], NEG)
        inter = b + m[..., None]
        m_t = jnp.maximum(inter, jnp.max(Dm, axis=-1))
        w = jnp.exp(Dm - m_t[..., None]) * jnp.einsum('bhtd,bhsd->bhts', qc, kc)
        wi = jnp.exp(inter - m_t)
        num = jnp.einsum('bhts,bhsv->bhtv', w, vc) + wi[..., None] * jnp.einsum('bhtd,bhdv->bhtv', qc, C)
        den = jnp.sum(w, axis=-1) + wi * jnp.einsum('bhtd,bhd->bht', qc, n)
        h = num / jnp.maximum(jnp.abs(den), jnp.exp(-m_t))[..., None]
        m_new = m_t[..., -1]
        ws = jnp.exp(b[..., -1:] - b + lic - m_new[..., None])
        dec = jnp.exp(b[..., -1] + m - m_new)
        C_new = dec[..., None, None] * C + jnp.einsum('bhsd,bhsv->bhdv', kc * ws[..., None], vc)
        n_new = dec[..., None] * n + jnp.einsum('bhsd,bhs->bhd', kc, ws)
        return (C_new, n_new, m_new), h

    xs = (_to_chunks(q, L), _to_chunks(k, L), _to_chunks(v, L), _to_chunks(log_i, L), _to_chunks(log_f, L))
    (C, n, m), hs = lax.scan(step, (C0, n0, m0), xs)
    return _from_chunks(hs)[:, :, :T], C, n, m


def hgrn2_chunked(q, k, v, log_f, S0):
    T = q.shape[2]
    L = min(HGRN_CHUNK, T)
    Tp = -(-T // L) * L
    q, k, v, log_f = (_pad_time(t, Tp, 0.0) for t in (q, k, v, log_f))
    causal = jnp.tril(jnp.ones((L, L), dtype=bool))[:, :, None]

    def step(S, xs):
        qc, kc, vc, lfc = xs
        b = jnp.cumsum(lfc, axis=2)
        diff = b[:, :, :, None, :] - b[:, :, None, :, :]
        decay = jnp.where(causal, jnp.exp(jnp.minimum(diff, 0.0)), 0.0)
        A = jnp.einsum('bhtd,bhsd,bhtsd->bhts', qc, kc, decay)
        o = jnp.einsum('bhts,bhsv->bhtv', A, vc) + jnp.einsum('bhtd,bhdv->bhtv', qc * jnp.exp(b), S)
        bL = b[:, :, -1:, :]
        S_new = jnp.exp(bL[:, :, 0, :])[..., None] * S + jnp.einsum('bhsd,bhsv->bhdv', kc * jnp.exp(bL - b), vc)
        return S_new, o

    xs = (_to_chunks(q, L), _to_chunks(k, L), _to_chunks(v, L), _to_chunks(log_f, L))
    S, os_ = lax.scan(step, S0, xs)
    return _from_chunks(os_)[:, :, :T], S


def mixer(h, w_in_l, b_gate_l, norm_a_l, lb_l, norm_b_l, w_out_l, C0, n0, m0, S0):
    B, T, _ = h.shape
    f32 = jnp.float32
    p = h @ w_in_l
    qa, ka, va, oa, ia, fa, qb, fb, ib, gb = jnp.split(p, SPLITS, axis=-1)

    def heads(t, H):
        return t.reshape(B, T, H, -1).transpose(0, 2, 1, 3).astype(f32)

    log_i = (ia + b_gate_l[:HA]).astype(f32).transpose(0, 2, 1)
    log_f = jax.nn.log_sigmoid((fa + b_gate_l[HA:]).astype(f32)).transpose(0, 2, 1)
    ha, C1, n1, m1 = mlstm_chunked(heads(qa, HA), heads(ka, HA) * DHA ** -0.5, heads(va, HA),
                                   log_i, log_f, C0.astype(f32), n0.astype(f32), m0.astype(f32))
    ha = rmsnorm(ha.transpose(0, 2, 1, 3), norm_a_l.reshape(HA, DHA)).reshape(B, T, W_A)
    ha = ha * jax.nn.sigmoid(oa.astype(f32))

    lb = lb_l.reshape(HB, DKB)[None, :, None, :]
    f_b = lb + (1.0 - lb) * jax.nn.sigmoid(heads(fb, HB))
    q_b = jax.nn.silu(heads(qb, HB)) * DKB ** -0.5
    hb, S1 = hgrn2_chunked(q_b, 1.0 - f_b, heads(ib, HB), jnp.log(f_b), S0.astype(f32))
    hb = rmsnorm(hb.transpose(0, 2, 1, 3), norm_b_l.reshape(HB, DVB)).reshape(B, T, W_B)
    hb = hb * jax.nn.silu(gb.astype(f32))

    out = jnp.concatenate([ha, hb], axis=-1).astype(h.dtype) @ w_out_l
    return out, C1, n1, m1, S1


def moe(h, router_w, router_b, w_gu, b_gu, w_down, b_down):
    N = h.shape[0]
    A = N * TOP_K
    logits = (h @ router_w + router_b).astype(jnp.float32)
    top_val, top_idx = lax.top_k(logits, TOP_K)
    gates = jax.nn.softmax(top_val, axis=-1)
    flat_e = top_idx.reshape(-1)
    flat_tok = jnp.arange(A, dtype=jnp.int32) // TOP_K
    order = jnp.argsort(flat_e, stable=True)
    sorted_e = flat_e[order]
    sorted_tok = flat_tok[order]
    counts = jnp.bincount(flat_e, length=N_EXPERTS)
    padded = ((counts + MOE_BLOCK - 1) // MOE_BLOCK) * MOE_BLOCK
    group_start = jnp.cumsum(counts) - counts
    pad_end = jnp.cumsum(padded)
    pad_start = pad_end - padded
    dest = pad_start[sorted_e] + (jnp.arange(A) - group_start[sorted_e])
    n_blocks = (A + N_EXPERTS * (MOE_BLOCK - 1) + MOE_BLOCK - 1) // MOE_BLOCK
    P = n_blocks * MOE_BLOCK
    x_pad = jnp.zeros((P, h.shape[1]), h.dtype).at[dest].set(h[sorted_tok])
    block_start = jnp.arange(n_blocks) * MOE_BLOCK
    block_e = jnp.minimum(jnp.searchsorted(pad_end, block_start, side='right'), N_EXPERTS - 1)

    def expert_block(args):
        xb, e = args
        gu = xb @ w_gu[e] + b_gu[e]
        g, u = gu[:, :D_FF], gu[:, D_FF:]
        g = jnp.minimum(g, SWIGLU_LIMIT)
        u = jnp.clip(u, -SWIGLU_LIMIT, SWIGLU_LIMIT)
        act = (u + 1.0) * (g * jax.nn.sigmoid(SWIGLU_ALPHA * g))
        return act @ w_down[e] + b_down[e]

    y_pad = lax.map(expert_block, (x_pad.reshape(n_blocks, MOE_BLOCK, -1), block_e)).reshape(P, -1)
    w_sorted = gates.reshape(-1)[order].astype(h.dtype)
    return jnp.zeros_like(h).at[sorted_tok].add(y_pad[dest] * w_sorted[:, None])


def trunk(x, c, C0, n0, m0, S0, w_ada, b_ada, norm_mix, norm_ffn, w_in, b_gate, norm_a, lb_logits,
          norm_b, w_out, router_w, router_b, w_gu, b_gu, w_down, b_down, norm_final):
    B, T, D = x.shape
    lb_all = jnp.cumsum(jax.nn.softmax(lb_logits.astype(jnp.float32), axis=0), axis=0)
    Cs, ns, ms, Ss = [], [], [], []
    for l in range(DEPTH):
        mod = (jax.nn.silu(c) @ w_ada[l] + b_ada[l])[:, None, :]
        sh1, sc1, g1, sh2, sc2, g2 = jnp.split(mod, 6, axis=-1)
        h = rmsnorm(x, norm_mix[l]) * (1.0 + sc1) + sh1
        mix, C1, n1, m1, S1 = mixer(h, w_in[l], b_gate[l], norm_a[l], lb_all[l], norm_b[l], w_out[l],
                                    C0[l], n0[l], m0[l], S0[l])
        x = x + g1 * mix
        h = rmsnorm(x, norm_ffn[l]) * (1.0 + sc2) + sh2
        ff = moe(h.reshape(B * T, D), router_w[l], router_b[l], w_gu[l], b_gu[l], w_down[l], b_down[l])
        x = x + g2 * ff.reshape(B, T, D)
        Cs.append(C1); ns.append(n1); ms.append(m1); Ss.append(S1)
    y = rmsnorm(x, norm_final)
    return y, jnp.stack(Cs), jnp.stack(ns), jnp.stack(ms), jnp.stack(Ss)


def setup_inputs(seed: int = 0) -> dict:
    key = jax.random.key(seed)
    ks = jax.random.split(key, 32)
    nrm = jax.random.normal
    D, E = D_MODEL, N_EXPERTS
    f_bias = jnp.linspace(3.0, 6.0, HA)[None, :] + 0.01 * nrm(ks[0], (DEPTH, HA))
    i_bias = 0.1 * nrm(ks[1], (DEPTH, HA))
    return {
        "x_prompt": nrm(ks[2], (BATCH, SEQ, D)),
        "x_sample": nrm(ks[3], (DEC_BATCH, DEC_SEQ, D)),
        "c_prompt": nrm(ks[4], (BATCH, D)),
        "c_sample": nrm(ks[5], (DEC_BATCH, D)),
        "state_mlstm_C": 0.5 * nrm(ks[6], (DEPTH, DEC_BATCH, HA, DHA, DHA)),
        "state_mlstm_n": jnp.abs(nrm(ks[7], (DEPTH, DEC_BATCH, HA, DHA))),
        "state_mlstm_m": nrm(ks[8], (DEPTH, DEC_BATCH, HA)),
        "state_hgrn_S": 0.5 * nrm(ks[9], (DEPTH, DEC_BATCH, HB, DKB, DVB)),
        "w_ada": 0.5 * D ** -0.5 * nrm(ks[10], (DEPTH, D, 6 * D)),
        "b_ada": 0.01 * nrm(ks[11], (DEPTH, 6 * D)),
        "norm_mix": 1.0 + 0.01 * nrm(ks[12], (DEPTH, D)),
        "norm_ffn": 1.0 + 0.01 * nrm(ks[13], (DEPTH, D)),
        "w_in": D ** -0.5 * nrm(ks[14], (DEPTH, D, PROJ_COLS)),
        "b_gate": jnp.concatenate([i_bias, f_bias], axis=-1),
        "norm_a": 1.0 + 0.01 * nrm(ks[15], (DEPTH, W_A)),
        "lb_logits": 0.1 * nrm(ks[16], (DEPTH + 1, HB * DKB)),
        "norm_b": 1.0 + 0.01 * nrm(ks[17], (DEPTH, W_B)),
        "w_out": D_MIX ** -0.5 * nrm(ks[18], (DEPTH, D_MIX, D)),
        "router_w": D ** -0.5 * nrm(ks[19], (DEPTH, D, E)),
        "router_b": 0.01 * nrm(ks[20], (DEPTH, E)),
        "w_gu": D ** -0.5 * nrm(ks[21], (DEPTH, E, D, 2 * D_FF)),
        "b_gu": 0.01 * nrm(ks[22], (DEPTH, E, 2 * D_FF)),
        "w_down": D_FF ** -0.5 * nrm(ks[23], (DEPTH, E, D_FF, D)),
        "b_down": 0.01 * nrm(ks[24], (DEPTH, E, D)),
        "norm_final": 1.0 + 0.01 * nrm(ks[25], (D,)),
    }


def reference(x_prompt, x_sample, c_prompt, c_sample, state_mlstm_C, state_mlstm_n, state_mlstm_m,
              state_hgrn_S, w_ada, b_ada, norm_mix, norm_ffn, w_in, b_gate, norm_a, lb_logits, norm_b,
              w_out, router_w, router_b, w_gu, b_gu, w_down, b_down, norm_final):
    f32 = jnp.float32
    Bp = x_prompt.shape[0]
    C0p = jnp.zeros((DEPTH, Bp, HA, DHA, DHA), f32)
    n0p = jnp.zeros((DEPTH, Bp, HA, DHA), f32)
    m0p = jnp.full((DEPTH, Bp, HA), NEG, f32)
    S0p = jnp.zeros((DEPTH, Bp, HB, DKB, DVB), f32)
    y_prompt, C_p, n_p, m_p, S_p = trunk(
        x_prompt, c_prompt, C0p, n0p, m0p, S0p, w_ada, b_ada, norm_mix, norm_ffn, w_in, b_gate, norm_a,
        lb_logits, norm_b, w_out, router_w, router_b, w_gu, b_gu, w_down, b_down, norm_final)
    y_sample, C_s, n_s, m_s, S_s = trunk(
        x_sample, c_sample, state_mlstm_C, state_mlstm_n, state_mlstm_m, state_hgrn_S, w_ada, b_ada,
        norm_mix, norm_ffn, w_in, b_gate, norm_a, lb_logits, norm_b, w_out, router_w, router_b, w_gu,
        b_gu, w_down, b_down, norm_final)
    return (y_prompt, y_sample, C_p, n_p, m_p, S_p, C_s, n_s, m_s, S_s)
```

```python
import functools

import jax
import jax.numpy as jnp
from jax import lax
from jax.experimental import pallas as pl
from jax.experimental.pallas import tpu as pltpu

F32 = jnp.float32
BF16 = jnp.bfloat16
I32 = jnp.int32

EPS = 1e-6
NEG = -1e30
SWIGLU_LIMIT = 7.0
SWIGLU_ALPHA = 1.702
TOP_K = 4

LANES = 128
HEAD_DIM = 128
N_HEADS = 4
GROUP_W = N_HEADS * HEAD_DIM
VMEM_LIMIT = 56 * 1024 * 1024

MLSTM_CHUNK = 128
HGRN_BLOCK = 128
HGRN_SUB = 16
TOKEN_TILE = 512
ROW_TILE = 128
EXPERT_BLOCK = 256
STEP_BATCH = 8


def _cparams(sem, vmem=VMEM_LIMIT):
    return pltpu.CompilerParams(dimension_semantics=sem, vmem_limit_bytes=vmem)


def _dot(a, b):
    return jnp.dot(a, b, preferred_element_type=F32)


def _dot_nt(a, b):
    return lax.dot_general(a, b, (((1,), (1,)), ((), ())), preferred_element_type=F32)


def _dot_tn(a, b):
    return lax.dot_general(a, b, (((0,), (0,)), ((), ())), preferred_element_type=F32)


def _sigmoid(x):
    return 1.0 / (1.0 + jnp.exp(-x))


def _log_sigmoid(x):
    return jnp.minimum(x, 0.0) - jnp.log1p(jnp.exp(-jnp.abs(x)))


def _rms(x, g):
    return x * lax.rsqrt(jnp.mean(x * x, axis=-1, keepdims=True) + EPS) * g


def _cumsum_rows(tri, x):
    hi = x.astype(BF16)
    r1 = x - hi.astype(F32)
    mid = r1.astype(BF16)
    lo = (r1 - mid.astype(F32)).astype(BF16)
    return _dot(tri, hi) + _dot(tri, mid) + _dot(tri, lo)


def _ada_kernel(c_ref, w_ref, b_ref, o_ref):
    c = c_ref[...]
    a = (c * _sigmoid(c)).astype(BF16)
    o_ref[...] = _dot(a, w_ref[...].astype(BF16)) + b_ref[...]


def _ada(c_all, w, b):
    m, d = c_all.shape
    n = w.shape[1]
    tn = 1024
    return pl.pallas_call(
        _ada_kernel,
        out_shape=jax.ShapeDtypeStruct((m, n), F32),
        grid=(n // tn,),
        in_specs=[pl.BlockSpec((m, d), lambda j: (0, 0)),
                  pl.BlockSpec((d, tn), lambda j: (0, j)),
                  pl.BlockSpec((1, tn), lambda j: (0, j))],
        out_specs=pl.BlockSpec((m, tn), lambda j: (0, j)),
        compiler_params=_cparams(("parallel",)),
        name="ada",
    )(c_all, w, b)


def _inproj_kernel(x_ref, sh_ref, sc_ref, nw_ref, w_ref, oa_ref, ob_ref):
    h = _rms(x_ref[...], nw_ref[...]) * (1.0 + sc_ref[...]) + sh_ref[...]
    hb = h.astype(BF16)
    na = oa_ref.shape[1]
    nb = ob_ref.shape[1]
    for j in range(0, na, GROUP_W):
        oa_ref[:, j:j + GROUP_W] = _dot(hb, w_ref[:, j:j + GROUP_W]).astype(oa_ref.dtype)
    ob_ref[:, 0:GROUP_W] = _dot(hb, w_ref[:, na:na + GROUP_W])
    ob_ref[:, GROUP_W:nb] = _dot(hb, w_ref[:, na + GROUP_W:na + nb])


def _inproj(x, mod_rows, mod_is_per_row, norm_w, w_r, n_a, tm, rows_per_batch, out_dtype):
    n, d = x.shape
    n_b = w_r.shape[1] - n_a
    if mod_is_per_row:
        sh_spec = pl.BlockSpec((tm, d), lambda i: (i, 0))
        sc_spec = pl.BlockSpec((tm, d), lambda i: (i, 1))
    else:
        tiles = rows_per_batch // tm
        sh_spec = pl.BlockSpec((None, 1, d), lambda i: (i // tiles, 0, 0))
        sc_spec = pl.BlockSpec((None, 1, d), lambda i: (i // tiles, 0, 1))
    return pl.pallas_call(
        _inproj_kernel,
        out_shape=(jax.ShapeDtypeStruct((n, n_a), out_dtype),
                   jax.ShapeDtypeStruct((n, n_b), F32)),
        grid=(n // tm,),
        in_specs=[pl.BlockSpec((tm, d), lambda i: (i, 0)), sh_spec, sc_spec,
                  pl.BlockSpec((1, d), lambda i: (0, 0)),
                  pl.BlockSpec(w_r.shape, lambda i: (0, 0))],
        out_specs=(pl.BlockSpec((tm, n_a), lambda i: (i, 0)),
                   pl.BlockSpec((tm, n_b), lambda i: (i, 0))),
        compiler_params=_cparams(("parallel",)),
        name="inproj",
    )(x, mod_rows, mod_rows, norm_w, w_r)


def _mlstm_kernel(q_ref, k_ref, v_ref, o_ref, g_ref, bg_ref, na_ref,
                  h_ref, c_ref, n_ref, m_ref):
    L = q_ref.shape[0]
    scale = HEAD_DIM ** -0.5

    @pl.when(pl.program_id(1) == 0)
    def _():
        c_ref[...] = jnp.zeros_like(c_ref)
        n_ref[...] = jnp.zeros_like(n_ref)
        m_ref[...] = jnp.full_like(m_ref, NEG)

    g = g_ref[...] + bg_ref[...]
    lane = lax.broadcasted_iota(I32, (L, LANES), 1)
    gates = jnp.where(lane < N_HEADS, g, _log_sigmoid(g))
    row = lax.broadcasted_iota(I32, (L, L), 0)
    col = lax.broadcasted_iota(I32, (L, L), 1)
    causal = row >= col
    tri = causal.astype(BF16)
    csum = _cumsum_rows(tri, gates)
    gates_t = gates.T
    csum_t = csum.T

    for h in range(N_HEADS):
        sl = slice(h * HEAD_DIM, (h + 1) * HEAD_DIM)
        qh, kh, vh = q_ref[:, sl], k_ref[:, sl], v_ref[:, sl]
        b_col = csum[:, N_HEADS + h:N_HEADS + h + 1]
        li_col = gates[:, h:h + 1]
        b_row = csum_t[N_HEADS + h:N_HEADS + h + 1, :]
        li_row = gates_t[h:h + 1, :]
        m_prev = m_ref[h:h + 1, 0:1]
        c_prev = c_ref[h]
        n_prev = n_ref[h:h + 1, :]

        dm = jnp.where(causal, b_col - b_row + li_row, NEG)
        inter = b_col + m_prev
        m_t = jnp.maximum(inter, jnp.max(dm, axis=-1, keepdims=True))
        w = jnp.exp(dm - m_t) * (_dot_nt(qh, kh) * scale)
        wi = jnp.exp(inter - m_t)
        num = _dot(w.astype(BF16), vh) + wi * _dot(qh, c_prev.astype(BF16))
        den = (jnp.sum(w, axis=-1, keepdims=True)
               + wi * jnp.sum(qh.astype(F32) * n_prev, axis=-1, keepdims=True))
        hval = num / jnp.maximum(jnp.abs(den), jnp.exp(-m_t))

        m_new = m_t[L - 1:L, :]
        b_last = b_col[L - 1:L, :]
        ws = jnp.exp(b_last - b_col + li_col - m_new)
        dec = jnp.exp(b_last + m_prev - m_new)
        kw = kh.astype(F32) * (ws * scale)
        c_ref[h] = dec * c_prev + _dot_tn(kw.astype(BF16), vh)
        n_ref[h:h + 1, :] = dec * n_prev + jnp.sum(kw, axis=0, keepdims=True)
        m_ref[h:h + 1, :] = jnp.broadcast_to(m_new, (1, LANES))

        hn = _rms(hval, na_ref[:, sl]) * _sigmoid(o_ref[:, sl].astype(F32))
        h_ref[:, sl] = hn.astype(h_ref.dtype)


def _mlstm_prompt(pa, pb, b_gate_row, norm_a, bsz, seq):
    L = MLSTM_CHUNK
    nc = seq // L
    n = bsz * seq
    gate_blk = (pb.shape[1] - LANES) // LANES

    def col(j):
        return pl.BlockSpec((L, GROUP_W), lambda b, c: (b * nc + c, j))

    return pl.pallas_call(
        _mlstm_kernel,
        out_shape=(jax.ShapeDtypeStruct((n, GROUP_W), BF16),
                   jax.ShapeDtypeStruct((1, bsz, N_HEADS, HEAD_DIM, HEAD_DIM), F32),
                   jax.ShapeDtypeStruct((1, bsz, N_HEADS, HEAD_DIM), F32),
                   jax.ShapeDtypeStruct((bsz, 8, LANES), F32)),
        grid=(bsz, nc),
        in_specs=[col(0), col(1), col(2), col(3),
                  pl.BlockSpec((L, LANES), lambda b, c: (b * nc + c, gate_blk)),
                  pl.BlockSpec((1, LANES), lambda b, c: (0, 0)),
                  pl.BlockSpec((1, GROUP_W), lambda b, c: (0, 0))],
        out_specs=(pl.BlockSpec((L, GROUP_W), lambda b, c: (b * nc + c, 0)),
                   pl.BlockSpec((None, None, N_HEADS, HEAD_DIM, HEAD_DIM), lambda b, c: (0, b, 0, 0, 0)),
                   pl.BlockSpec((None, None, N_HEADS, HEAD_DIM), lambda b, c: (0, b, 0, 0)),
                   pl.BlockSpec((None, 8, LANES), lambda b, c: (b, 0, 0))),
        compiler_params=_cparams(("parallel", "arbitrary")),
        name="mlstm_prompt",
    )(pa, pa, pa, pa, pb, b_gate_row, norm_a)


def _lower_bound(lb_logits_ref):
    lg = lb_logits_ref[...]
    e = jnp.exp(lg - jnp.max(lg, axis=0, keepdims=True))
    return e[0:1, :] / jnp.sum(e, axis=0, keepdims=True)


def _hgrn_kernel(q_ref, v_ref, g_ref, f_ref, lbl_ref, nb_ref,
                 h_ref, s_ref, st_ref, qs_ref, ks_ref, bs_ref, os_ref):
    LB = q_ref.shape[0]
    C = HGRN_SUB
    scale = HEAD_DIM ** -0.5

    @pl.when(pl.program_id(1) == 0)
    def _():
        st_ref[...] = jnp.zeros_like(st_ref)

    lb = _lower_bound(lbl_ref)
    f = lb + (1.0 - lb) * _sigmoid(f_ref[...])
    qraw = q_ref[...].astype(F32)
    qs_ref[...] = qraw * _sigmoid(qraw) * scale
    ks_ref[...] = 1.0 - f
    row = lax.broadcasted_iota(I32, (LB, LB), 0)
    col = lax.broadcasted_iota(I32, (LB, LB), 1)
    shift = C.bit_length() - 1
    tri = ((row >= col) & ((row >> shift) == (col >> shift))).astype(BF16)
    bs_ref[...] = _cumsum_rows(tri, jnp.log(f))

    r16 = lax.broadcasted_iota(I32, (C, C), 0)
    c16 = lax.broadcasted_iota(I32, (C, C), 1)

    def sub_chunk(i, carry):
        r0 = pl.multiple_of(i * C, C)
        rows = pl.ds(r0, C)
        for h in range(N_HEADS):
            sl = slice(h * HEAD_DIM, (h + 1) * HEAD_DIM)
            q_i = qs_ref[rows, sl]
            k_i = ks_ref[rows, sl]
            b_i = bs_ref[rows, sl]
            v_i = v_ref[rows, sl]
            st = st_ref[h]
            a = jnp.zeros((C, C), F32)
            for s in range(C):
                p = q_i * k_i[s:s + 1, :] * jnp.exp(jnp.minimum(b_i - b_i[s:s + 1, :], 0.0))
                a = jnp.where(c16 == s, jnp.sum(p, axis=-1, keepdims=True), a)
            a = jnp.where(r16 >= c16, a, 0.0)
            o = _dot(a.astype(BF16), v_i) + _dot_nt((q_i * jnp.exp(b_i)).astype(BF16), st.astype(BF16))
            os_ref[rows, sl] = o
            b_l = b_i[C - 1:C, :]
            kd = k_i * jnp.exp(b_l - b_i)
            st_ref[h] = st * jnp.exp(b_l) + _dot_tn(v_i, kd.astype(BF16))
        return carry

    lax.fori_loop(0, LB // C, sub_chunk, 0)

    for h in range(N_HEADS):
        sl = slice(h * HEAD_DIM, (h + 1) * HEAD_DIM)
        gv = g_ref[:, sl].astype(F32)
        hn = _rms(os_ref[:, sl], nb_ref[:, sl]) * (gv * _sigmoid(gv))
        h_ref[:, sl] = hn.astype(h_ref.dtype)

    @pl.when(pl.program_id(1) == pl.num_programs(1) - 1)
    def _():
        for h in range(N_HEADS):
            s_ref[h] = st_ref[h].T


def _hgrn_prompt(pa, pb, lb_logits, norm_b, bsz, seq):
    LB = HGRN_BLOCK
    nc = seq // LB
    n = bsz * seq

    def col(j):
        return pl.BlockSpec((LB, GROUP_W), lambda b, c: (b * nc + c, j))

    return pl.pallas_call(
        _hgrn_kernel,
        out_shape=(jax.ShapeDtypeStruct((n, GROUP_W), BF16),
                   jax.ShapeDtypeStruct((1, bsz, N_HEADS, HEAD_DIM, HEAD_DIM), F32)),
        grid=(bsz, nc),
        in_specs=[col(4), col(5), col(6),
                  pl.BlockSpec((LB, GROUP_W), lambda b, c: (b * nc + c, 0)),
                  pl.BlockSpec(lb_logits.shape, lambda b, c: (0, 0)),
                  pl.BlockSpec((1, GROUP_W), lambda b, c: (0, 0))],
        out_specs=(pl.BlockSpec((LB, GROUP_W), lambda b, c: (b * nc + c, 0)),
                   pl.BlockSpec((None, None, N_HEADS, HEAD_DIM, HEAD_DIM), lambda b, c: (0, b, 0, 0, 0))),
        scratch_shapes=[pltpu.VMEM((N_HEADS, HEAD_DIM, HEAD_DIM), F32)]
                       + [pltpu.VMEM((LB, GROUP_W), F32)] * 4,
        compiler_params=_cparams(("parallel", "arbitrary")),
        name="hgrn_prompt",
    )(pa, pa, pa, pb, lb_logits, norm_b)


def _step_kernel(pa_ref, pb_ref, bg_ref, na_ref, nb_ref, lbl_ref,
                 c0_ref, n0_ref, m0_ref, s0_ref,
                 ha_ref, hb_ref, c1_ref, n1_ref, m1_ref, s1_ref, ta_ref, tb_ref):
    scale = HEAD_DIM ** -0.5
    W = GROUP_W
    H = N_HEADS
    D = HEAD_DIM
    lb = _lower_bound(lbl_ref)
    gates_all = pb_ref[:, W:W + LANES] + bg_ref[...]
    f_all = lb + (1.0 - lb) * _sigmoid(pb_ref[:, 0:W])
    ta_ref[...] = jnp.zeros_like(ta_ref)
    tb_ref[...] = jnp.zeros_like(tb_ref)

    aux_a, aux_b = {}, {}
    for j in range(STEP_BATCH):
        row = slice(j, j + 1)
        for h in range(H):
            q = pa_ref[row, h * D:(h + 1) * D]
            k = pa_ref[row, W + h * D:W + (h + 1) * D]
            li = gates_all[row, h:h + 1]
            lf = _log_sigmoid(gates_all[row, H + h:H + h + 1])
            inter = lf + m0_ref[row, h:h + 1]
            m_t = jnp.maximum(inter, li)
            ws = jnp.exp(li - m_t)
            dec = jnp.exp(inter - m_t)
            kw = k * (ws * scale)
            ta_ref[2 * H * j + h:2 * H * j + h + 1, :] = q
            ta_ref[2 * H * j + H + h:2 * H * j + H + h + 1, :] = kw
            aux_a[j, h] = (q, k, kw, m_t, ws, dec)

            qraw = pa_ref[row, 4 * W + h * D:4 * W + (h + 1) * D]
            qb = qraw * _sigmoid(qraw) * scale
            f = f_all[row, h * D:(h + 1) * D]
            decay = jnp.exp(jnp.log(f))
            kb = 1.0 - f
            tb_ref[3 * H * j + h:3 * H * j + h + 1, :] = qb * decay
            tb_ref[3 * H * j + H + h:3 * H * j + H + h + 1, :] = decay
            tb_ref[3 * H * j + 2 * H + h:3 * H * j + 2 * H + h + 1, :] = kb
            aux_b[j, h] = (qb, kb)

    ta = ta_ref[...].T
    tb = tb_ref[...].T
    for j in range(STEP_BATCH):
        row = slice(j, j + 1)
        for h in range(H):
            sl = slice(h * D, (h + 1) * D)
            q, k, kw, m_t, ws, dec = aux_a[j, h]
            v = pa_ref[row, 2 * W + h * D:2 * W + (h + 1) * D]
            og = pa_ref[row, 3 * W + h * D:3 * W + (h + 1) * D]
            q_col = ta[:, 2 * H * j + h:2 * H * j + h + 1]
            kw_col = ta[:, 2 * H * j + H + h:2 * H * j + H + h + 1]
            c0 = c0_ref[j, h]
            n0 = n0_ref[j, h:h + 1, :]
            w = ws * (jnp.sum(q * k, axis=-1, keepdims=True) * scale)
            num = w * v + dec * jnp.sum(q_col * c0, axis=0, keepdims=True)
            den = w + dec * jnp.sum(q * n0, axis=-1, keepdims=True)
            hval = num / jnp.maximum(jnp.abs(den), jnp.exp(-m_t))
            c1_ref[j, h] = dec * c0 + kw_col * v
            n1_ref[j, h:h + 1, :] = dec * n0 + kw
            m1_ref[row, h:h + 1] = m_t
            ha_ref[row, sl] = _rms(hval, na_ref[:, sl]) * _sigmoid(og)
            qb, kb = aux_b[j, h]
            vb = pa_ref[row, 5 * W + h * D:5 * W + (h + 1) * D]
            gv = pa_ref[row, 6 * W + h * D:6 * W + (h + 1) * D]
            qd_col = tb[:, 3 * H * j + h:3 * H * j + h + 1]
            d_col = tb[:, 3 * H * j + H + h:3 * H * j + H + h + 1]
            k_col = tb[:, 3 * H * j + 2 * H + h:3 * H * j + 2 * H + h + 1]
            s0 = s0_ref[j, h]
            a = jnp.sum(qb * kb, axis=-1, keepdims=True)
            o = a * vb + jnp.sum(qd_col * s0, axis=0, keepdims=True)
            s1_ref[j, h] = d_col * s0 + k_col * vb
            hb_ref[row, sl] = _rms(o, nb_ref[:, sl]) * (gv * _sigmoid(gv))


def _step_mixers(pa, pb, b_gate_row, norm_a, norm_b, lb_logits, c0, n0, m0, s0):
    bs = pa.shape[0]
    sb = STEP_BATCH
    st5 = pl.BlockSpec((None, sb, N_HEADS, HEAD_DIM, HEAD_DIM), lambda i: (0, i, 0, 0, 0))
    st4 = pl.BlockSpec((None, sb, N_HEADS, HEAD_DIM), lambda i: (0, i, 0, 0))
    st3 = pl.BlockSpec((None, sb, N_HEADS), lambda i: (0, i, 0))
    rowblk = lambda w: pl.BlockSpec((sb, w), lambda i: (i, 0))
    const = lambda a: pl.BlockSpec(a.shape, lambda i: (0,) * a.ndim)
    return pl.pallas_call(
        _step_kernel,
        out_shape=(jax.ShapeDtypeStruct((bs, GROUP_W), F32),
                   jax.ShapeDtypeStruct((bs, GROUP_W), F32),
                   jax.ShapeDtypeStruct(c0.shape, F32),
                   jax.ShapeDtypeStruct(n0.shape, F32),
                   jax.ShapeDtypeStruct(m0.shape, F32),
                   jax.ShapeDtypeStruct(s0.shape, F32)),
        grid=(bs // sb,),
        in_specs=[rowblk(pa.shape[1]), rowblk(pb.shape[1]), const(b_gate_row), const(norm_a),
                  const(norm_b), const(lb_logits), st5, st4, st3, st5],
        out_specs=(rowblk(GROUP_W), rowblk(GROUP_W), st5, st4, st3, st5),
        scratch_shapes=[pltpu.VMEM((LANES, LANES), F32), pltpu.VMEM((LANES, LANES), F32)],
        compiler_params=_cparams(("parallel",)),
        name="step_mixers",
    )(pa, pb, b_gate_row, norm_a, norm_b, lb_logits, c0, n0, m0, s0)


def _post_rows(ha_ref, hb_ref, x_ref, g1_ref, sh2_ref, sc2_ref, nf_ref, wo_ref, rw_ref, rb_ref,
               x1_ref, h2_ref, idx_ref, gate_ref):
    tm = x_ref.shape[0]
    mix = (_dot(ha_ref[...].astype(BF16), wo_ref[0:GROUP_W, :])
           + _dot(hb_ref[...].astype(BF16), wo_ref[GROUP_W:2 * GROUP_W, :]))
    x1 = x_ref[...] + g1_ref[...] * mix
    x1_ref[0:tm, :] = x1
    h2 = _rms(x1, nf_ref[...]) * (1.0 + sc2_ref[...]) + sh2_ref[...]
    h2_ref[0:tm, :] = h2
    logits = _dot(h2.astype(BF16), rw_ref[...]) + rb_ref[...]
    lane = lax.broadcasted_iota(I32, (tm, LANES), 1)
    lane_f = lane.astype(F32)
    cur = logits
    vals, ids = [], []
    for _ in range(TOP_K):
        mx = jnp.max(cur, axis=-1, keepdims=True)
        am = jnp.min(jnp.where(cur == mx, lane_f, float(LANES)), axis=-1, keepdims=True)
        vals.append(mx)
        ids.append(am)
        cur = jnp.where(lane_f == am, -jnp.inf, cur)
    es = [jnp.exp(v - vals[0]) for v in vals]
    tot = es[0] + es[1] + es[2] + es[3]
    idx_out = jnp.full((tm, LANES), -1.0, F32)
    gate_out = jnp.zeros((tm, LANES), F32)
    for k in range(TOP_K):
        idx_out = jnp.where(lane == k, ids[k], idx_out)
        gate_out = jnp.where(lane == k, es[k] / tot, gate_out)
    idx_ref[0:tm, :] = idx_out.astype(I32)
    gate_ref[0:tm, :] = gate_out


def _post_kernel(ha_ref, hb_ref, x_ref, g1_ref, sh2_ref, sc2_ref,
                 has_ref, hbs_ref, xs_ref, g1s_ref, sh2s_ref, sc2s_ref,
                 nf_ref, wo_ref, rw_ref, rb_ref, x1_ref, h2_ref, idx_ref, gate_ref, *, n_prompt_tiles):
    i = pl.program_id(0)
    shared = (nf_ref, wo_ref, rw_ref, rb_ref, x1_ref, h2_ref, idx_ref, gate_ref)

    @pl.when(i < n_prompt_tiles)
    def _():
        _post_rows(ha_ref, hb_ref, x_ref, g1_ref, sh2_ref, sc2_ref, *shared)

    @pl.when(i == n_prompt_tiles)
    def _():
        _post_rows(has_ref, hbs_ref, xs_ref, g1s_ref, sh2s_ref, sc2s_ref, *shared)


def _post(ha_p, hb_p, x_p, mod_p, ha_s, hb_s, x_s, mod_s, norm_ffn, w_out, rw, rb, tm, rows_per_batch):
    n_p, d = x_p.shape
    n_s = x_s.shape[0]
    assert n_s <= tm
    npt = n_p // tm
    tiles = rows_per_batch // tm
    nbatch = n_p // rows_per_batch
    n_all = n_p + n_s
    prow = lambda w: pl.BlockSpec((tm, w), lambda i: (jnp.minimum(i, npt - 1), 0))
    pmod = lambda j: pl.BlockSpec((None, 1, d), lambda i: (jnp.minimum(i // tiles, nbatch - 1), 0, j))
    smod = lambda j: pl.BlockSpec((n_s, d), lambda i: (0, j))
    const = lambda a: pl.BlockSpec(a.shape, lambda i: (0,) * a.ndim)
    out_blk = lambda w: pl.BlockSpec((tm, w), lambda i: (i, 0))
    return pl.pallas_call(
        functools.partial(_post_kernel, n_prompt_tiles=npt),
        out_shape=(jax.ShapeDtypeStruct((n_all, d), F32),
                   jax.ShapeDtypeStruct((n_all, d), F32),
                   jax.ShapeDtypeStruct((n_all, LANES), I32),
                   jax.ShapeDtypeStruct((n_all, LANES), F32)),
        grid=(npt + 1,),
        in_specs=[prow(GROUP_W), prow(GROUP_W), prow(d), pmod(2), pmod(3), pmod(4),
                  const(ha_s), const(hb_s), const(x_s), smod(2), smod(3), smod(4),
                  const(norm_ffn), const(w_out), const(rw), const(rb)],
        out_specs=(out_blk(d), out_blk(d), out_blk(LANES), out_blk(LANES)),
        compiler_params=_cparams(("parallel",)),
        name="post",
    )(ha_p, hb_p, x_p, mod_p, mod_p, mod_p, ha_s, hb_s, x_s, mod_s, mod_s, mod_s, norm_ffn, w_out, rw, rb)


def _rank_kernel(idx_ref, rank_ref, cnt_ref, base_ref):
    R = idx_ref.shape[0]

    @pl.when(pl.program_id(0) == 0)
    def _():
        base_ref[...] = jnp.zeros_like(base_ref)

    idx = idx_ref[...]
    lane = lax.broadcasted_iota(I32, (R, LANES), 1)
    hits = [lane == idx[:, k:k + 1] for k in range(TOP_K)]
    onehot = jnp.zeros((R, LANES), F32)
    for hk in hits:
        onehot = onehot + hk.astype(F32)
    row = lax.broadcasted_iota(I32, (R, R), 0)
    col = lax.broadcasted_iota(I32, (R, R), 1)
    before = _dot((row > col).astype(BF16), onehot.astype(BF16)) + base_ref[...]
    rank = jnp.zeros((R, LANES), F32)
    for k, hk in enumerate(hits):
        rk = jnp.sum(jnp.where(hk, before, 0.0), axis=-1, keepdims=True)
        rank = jnp.where(lane == k, rk, rank)
    rank_ref[...] = rank
    base = base_ref[...] + jnp.sum(onehot, axis=0, keepdims=True)
    base_ref[...] = base
    cnt_ref[...] = base


def _rank(idx):
    n = idx.shape[0]
    R = ROW_TILE
    return pl.pallas_call(
        _rank_kernel,
        out_shape=(jax.ShapeDtypeStruct((n, LANES), F32), jax.ShapeDtypeStruct((1, LANES), F32)),
        grid=(n // R,),
        in_specs=[pl.BlockSpec((R, LANES), lambda i: (i, 0))],
        out_specs=(pl.BlockSpec((R, LANES), lambda i: (i, 0)), pl.BlockSpec((1, LANES), lambda i: (0, 0))),
        scratch_shapes=[pltpu.VMEM((1, LANES), F32)],
        compiler_params=_cparams(("arbitrary",)),
        name="rank",
    )(idx)


def _dest_kernel(idx_ref, rank_ref, start_ref, dest_ref):
    R = idx_ref.shape[0]
    idx = idx_ref[...]
    lane = lax.broadcasted_iota(I32, (R, LANES), 1)
    dest = rank_ref[...]
    for k in range(TOP_K):
        st = jnp.sum(jnp.where(lane == idx[:, k:k + 1], start_ref[...], 0.0), axis=-1, keepdims=True)
        dest = dest + jnp.where(lane == k, st, 0.0)
    dest_ref[...] = dest.T[0:8, :].astype(I32)


def _dest(idx, rank, start_row):
    n = idx.shape[0]
    R = ROW_TILE
    return pl.pallas_call(
        _dest_kernel,
        out_shape=jax.ShapeDtypeStruct((n // R, 8, R), I32),
        grid=(n // R,),
        in_specs=[pl.BlockSpec((R, LANES), lambda i: (i, 0)),
                  pl.BlockSpec((R, LANES), lambda i: (i, 0)),
                  pl.BlockSpec((1, LANES), lambda i: (0, 0))],
        out_specs=pl.BlockSpec((None, 8, R), lambda i: (i, 0, 0)),
        compiler_params=_cparams(("parallel",)),
        name="dest",
    )(idx, rank, start_row)


def _dispatch_kernel(zb_ref, dest_hbm, x_ref, xs_hbm, dest_smem, zeros, sem_idx, sem_rows, sem_zero):
    R = x_ref.shape[0]
    bm = zeros.shape[0]
    i = pl.program_id(0)

    @pl.when(i == 0)
    def _():
        zeros[...] = jnp.zeros_like(zeros)

        def zero_copy(j):
            return pltpu.make_async_copy(zeros, xs_hbm.at[pl.ds(pl.multiple_of(zb_ref[j] * bm, bm), bm)], sem_zero)

        def start(j, c):
            @pl.when(zb_ref[j] >= 0)
            def _():
                zero_copy(j).start()
            return c

        def wait(j, c):
            @pl.when(zb_ref[j] >= 0)
            def _():
                zero_copy(j).wait()
            return c

        lax.fori_loop(0, zb_ref.shape[0], start, 0)
        lax.fori_loop(0, zb_ref.shape[0], wait, 0)

    load = pltpu.make_async_copy(dest_hbm.at[i], dest_smem, sem_idx)
    load.start()
    load.wait()

    def row_copy(t, k):
        return pltpu.make_async_copy(x_ref.at[pl.ds(t, 1)], xs_hbm.at[pl.ds(dest_smem[k, t], 1)], sem_rows)

    def issue(t, c):
        for k in range(TOP_K):
            row_copy(t, k).start()
        return c

    lax.fori_loop(0, R, issue, 0)

    def drain(t, c):
        for k in range(TOP_K):
            row_copy(t, k).wait()
        return c

    lax.fori_loop(0, R, drain, 0)


def _dispatch(zero_blocks, dest, h2, p_rows):
    n, d = h2.shape
    R = ROW_TILE
    return pl.pallas_call(
        _dispatch_kernel,
        out_shape=jax.ShapeDtypeStruct((p_rows, d), F32),
        grid_spec=pltpu.PrefetchScalarGridSpec(
            num_scalar_prefetch=1,
            grid=(n // R,),
            in_specs=[pl.BlockSpec(memory_space=pl.ANY),
                      pl.BlockSpec((R, d), lambda i, zb: (i, 0))],
            out_specs=pl.BlockSpec(memory_space=pl.ANY),
            scratch_shapes=[pltpu.SMEM((8, R), I32), pltpu.VMEM((EXPERT_BLOCK, d), F32),
                            pltpu.SemaphoreType.DMA, pltpu.SemaphoreType.DMA, pltpu.SemaphoreType.DMA]),
        compiler_params=_cparams(("arbitrary",)),
        name="dispatch",
    )(zero_blocks, dest, h2)


def _expert_kernel(be_ref, nv_ref, x_ref, wgu_ref, bgu_ref, wd_ref, bd_ref, y_ref, wgu_bf, wd_bf):
    i = pl.program_id(0)
    dff = wd_ref.shape[0]

    @pl.when(i < nv_ref[0])
    def _():
        changed = jnp.logical_or(i == 0, be_ref[i] != be_ref[jnp.maximum(i - 1, 0)])

        @pl.when(changed)
        def _():
            rows = 128

            def cast_gu(r, c):
                sl = pl.ds(pl.multiple_of(r * rows, rows), rows)
                wgu_bf[sl, :] = wgu_ref[sl, :].astype(BF16)
                return c

            def cast_d(r, c):
                sl = pl.ds(pl.multiple_of(r * rows, rows), rows)
                wd_bf[sl, :] = wd_ref[sl, :].astype(BF16)
                return c

            lax.fori_loop(0, wgu_ref.shape[0] // rows, cast_gu, 0)
            lax.fori_loop(0, wd_ref.shape[0] // rows, cast_d, 0)

        x = x_ref[...].astype(BF16)
        g = jnp.minimum(_dot(x, wgu_bf[:, 0:dff]) + bgu_ref[:, 0:dff], SWIGLU_LIMIT)
        u = jnp.clip(_dot(x, wgu_bf[:, dff:2 * dff]) + bgu_ref[:, dff:2 * dff], -SWIGLU_LIMIT, SWIGLU_LIMIT)
        act = (u + 1.0) * (g * _sigmoid(SWIGLU_ALPHA * g))
        y_ref[...] = _dot(act.astype(BF16), wd_bf[...]) + bd_ref[...]

    @pl.when(i >= nv_ref[0])
    def _():
        y_ref[...] = jnp.zeros_like(y_ref)


def _experts(block_e, n_valid, xs, w_gu, b_gu, w_down, b_down):
    p_rows, d = xs.shape
    bm = EXPERT_BLOCK
    nb = p_rows // bm
    e, _, dff2 = w_gu.shape[1:]
    dff = w_down.shape[2]
    blk = lambda i, be, nv: (jnp.minimum(i, nv[0] - 1), 0)
    return pl.pallas_call(
        _expert_kernel,
        out_shape=jax.ShapeDtypeStruct((p_rows, d), F32),
        grid_spec=pltpu.PrefetchScalarGridSpec(
            num_scalar_prefetch=2,
            grid=(nb,),
            in_specs=[pl.BlockSpec((bm, d), blk),
                      pl.BlockSpec((None, None, d, dff2), lambda i, be, nv: (0, be[i], 0, 0)),
                      pl.BlockSpec((None, 1, dff2), lambda i, be, nv: (be[i], 0, 0)),
                      pl.BlockSpec((None, None, dff, d), lambda i, be, nv: (0, be[i], 0, 0)),
                      pl.BlockSpec((None, 1, d), lambda i, be, nv: (be[i], 0, 0))],
            out_specs=pl.BlockSpec((bm, d), lambda i, be, nv: (i, 0)),
            scratch_shapes=[pltpu.VMEM((d, dff2), BF16), pltpu.VMEM((dff, d), BF16)]),
        compiler_params=_cparams(("arbitrary",)),
        name="experts",
    )(block_e, n_valid, xs, w_gu, b_gu.reshape(e, 1, dff2), w_down, b_down.reshape(e, 1, d))


def _combine_kernel(dest_hbm, ys_hbm, x1_ref, gate_ref, g2p_ref, g2s_ref, nf_ref,
                    yp_ref, ysm_ref, dest_smem, buf, sem_idx, sem_rows, *, n_prompt_tiles):
    R = x1_ref.shape[0]
    i = pl.program_id(0)
    load = pltpu.make_async_copy(dest_hbm.at[i], dest_smem, sem_idx)
    load.start()
    load.wait()

    def row_copy(t, k):
        return pltpu.make_async_copy(ys_hbm.at[pl.ds(dest_smem[k, t], 1)], buf.at[k, pl.ds(t, 1)], sem_rows)

    def issue(t, c):
        for k in range(TOP_K):
            row_copy(t, k).start()
        return c

    lax.fori_loop(0, R, issue, 0)

    def drain(t, c):
        for k in range(TOP_K):
            row_copy(t, k).wait()
        return c

    lax.fori_loop(0, R, drain, 0)

    gate = gate_ref[...]
    ff = jnp.zeros(x1_ref.shape, F32)
    for k in range(TOP_K):
        ff = ff + buf[k] * gate[:, k:k + 1]
    is_prompt = i < n_prompt_tiles

    @pl.when(is_prompt)
    def _():
        yp_ref[...] = _rms(x1_ref[...] + g2p_ref[...] * ff, nf_ref[...])

    @pl.when(jnp.logical_not(is_prompt))
    def _():
        ysm_ref[...] = _rms(x1_ref[...] + g2s_ref[...] * ff, nf_ref[...])


def _combine(dest, ys, x1, gates, mod_p, mod_s, norm_final, n_prompt, rows_per_batch):
    n, d = x1.shape
    R = ROW_TILE
    npt = n_prompt // R
    tiles = rows_per_batch // R
    nbatch = n_prompt // rows_per_batch
    n_s = n - n_prompt
    return pl.pallas_call(
        functools.partial(_combine_kernel, n_prompt_tiles=npt),
        out_shape=(jax.ShapeDtypeStruct((n_prompt, d), F32), jax.ShapeDtypeStruct((n_s, d), F32)),
        grid=(n // R,),
        in_specs=[pl.BlockSpec(memory_space=pl.ANY),
                  pl.BlockSpec(memory_space=pl.ANY),
                  pl.BlockSpec((R, d), lambda i: (i, 0)),
                  pl.BlockSpec((R, LANES), lambda i: (i, 0)),
                  pl.BlockSpec((None, 1, d), lambda i: (jnp.minimum(i // tiles, nbatch - 1), 0, 5)),
                  pl.BlockSpec((R, d), lambda i: (jnp.maximum(i - npt, 0), 5)),
                  pl.BlockSpec((1, d), lambda i: (0, 0))],
        out_specs=(pl.BlockSpec((R, d), lambda i: (jnp.minimum(i, npt - 1), 0)),
                   pl.BlockSpec((R, d), lambda i: (jnp.maximum(i - npt, 0), 0))),
        scratch_shapes=[pltpu.SMEM((8, R), I32), pltpu.VMEM((TOP_K, R, d), F32),
                        pltpu.SemaphoreType.DMA, pltpu.SemaphoreType.DMA],
        compiler_params=_cparams(("arbitrary",)),
        name="combine",
    )(dest, ys, x1, gates, mod_p, mod_s, norm_final)


def _reorder_w_in(w):
    W = GROUP_W
    g0 = 4 * W
    g1 = g0 + 2 * N_HEADS
    pad = jnp.zeros((w.shape[0], LANES - 2 * N_HEADS), w.dtype)
    cols = [w[:, 0:g0], w[:, g1:g1 + W], w[:, g1 + 2 * W:g1 + 4 * W], w[:, g1 + W:g1 + 2 * W], w[:, g0:g1], pad]
    return jnp.concatenate(cols, axis=1).astype(BF16)


def kernel(x_prompt, x_sample, c_prompt, c_sample, state_mlstm_C, state_mlstm_n, state_mlstm_m,
           state_hgrn_S, w_ada, b_ada, norm_mix, norm_ffn, w_in, b_gate, norm_a, lb_logits, norm_b,
           w_out, router_w, router_b, w_gu, b_gu, w_down, b_down, norm_final):
    bp, seq, d = x_prompt.shape
    bs = x_sample.shape[0]
    assert x_sample.shape[1] == 1 and w_ada.shape[0] == 1
    n_p = bp * seq
    n_all = n_p + bs
    n_exp = router_w.shape[2]
    W = GROUP_W
    n_a = 7 * W

    mod = _ada(jnp.concatenate([c_prompt, c_sample], axis=0), w_ada[0], b_ada)
    mod_p = mod[:bp].reshape(bp, 1, 6 * d)
    mod_s = mod[bp:]

    w_r = _reorder_w_in(w_in[0])
    nmix = norm_mix.reshape(1, d)
    tm = min(TOKEN_TILE, seq)
    xp = x_prompt.reshape(n_p, d)
    xs_ = x_sample.reshape(bs, d)
    pa_p, pb_p = _inproj(xp, mod_p, False, nmix, w_r, n_a, tm, seq, BF16)
    pa_s, pb_s = _inproj(xs_, mod_s, True, nmix, w_r, n_a, bs, None, F32)

    bg_row = jnp.pad(b_gate.reshape(1, 2 * N_HEADS), ((0, 0), (0, LANES - 2 * N_HEADS)))
    na = norm_a.reshape(1, W)
    nb_ = norm_b.reshape(1, W)
    ha_p, c_p, nrm_p, m_p = _mlstm_prompt(pa_p, pb_p, bg_row, na, bp, seq)
    hb_p, s_p = _hgrn_prompt(pa_p, pb_p, lb_logits, nb_, bp, seq)
    ha_s, hb_s, c_s, nrm_s, m_s, s_s = _step_mixers(
        pa_s, pb_s, bg_row, na, nb_, lb_logits, state_mlstm_C, state_mlstm_n, state_mlstm_m, state_hgrn_S)

    w_o = w_out[0].astype(BF16)
    rw = jnp.pad(router_w[0], ((0, 0), (0, LANES - n_exp))).astype(BF16)
    rb = jnp.pad(router_b.reshape(1, n_exp), ((0, 0), (0, LANES - n_exp)), constant_values=NEG)
    nffn = norm_ffn.reshape(1, d)
    x1, h2, idx, gates = _post(ha_p, hb_p, xp, mod_p, ha_s, hb_s, xs_, mod_s, nffn, w_o, rw, rb, tm, seq)

    bm = EXPERT_BLOCK
    rank, counts = _rank(idx)
    cnt = counts[0].astype(I32)
    padded = ((cnt + bm - 1) // bm) * bm
    ends = jnp.cumsum(padded)
    start_row = (ends - padded).astype(F32).reshape(1, LANES)
    dest = _dest(idx, rank, start_row)
    n_blocks = (n_all * TOP_K + n_exp * (bm - 1) + bm - 1) // bm
    n_valid = (ends[n_exp - 1] // bm).astype(I32)
    blk_start = jnp.arange(n_blocks, dtype=I32) * bm
    block_e = jnp.minimum(jnp.searchsorted(ends[:n_exp], blk_start, side='right'), n_exp - 1).astype(I32)
    last_e = block_e[jnp.maximum(n_valid - 1, 0)]
    block_e = jnp.where(jnp.arange(n_blocks) < n_valid, block_e, last_e)

    min_valid = -(-(n_all * TOP_K) // bm)
    group_last = jnp.where(padded[:n_exp] > 0, ends[:n_exp] // bm - 1, -1)
    tail = n_valid + jnp.arange(n_blocks - min_valid, dtype=I32)
    tail = jnp.where(tail < n_blocks, tail, -1)
    zero_blocks = jnp.concatenate([group_last.astype(I32), tail])

    xs_sorted = _dispatch(zero_blocks, dest, h2, n_blocks * bm)
    ys_sorted = _experts(block_e, n_valid.reshape(1), xs_sorted, w_gu, b_gu[0], w_down, b_down[0])
    y_p, y_s = _combine(dest, ys_sorted, x1, gates, mod_p, mod_s, norm_final.reshape(1, d), n_p, seq)

    m_p_out = m_p[:, :N_HEADS, 0][None]
    return (y_p.reshape(bp, seq, d), y_s.reshape(bs, 1, d), c_p, nrm_p, m_p_out, s_p,
            c_s, nrm_s, m_s, s_s)
```

```python
import functools

import jax
import jax.numpy as jnp
from jax import lax
from jax.experimental import pallas as pl
from jax.experimental.pallas import tpu as pltpu

F32 = jnp.float32
BF16 = jnp.bfloat16
I32 = jnp.int32

EPS = 1e-6
NEG = -1e30
SWIGLU_LIMIT = 7.0
SWIGLU_ALPHA = 1.702
TOP_K = 4

LANES = 128
HEAD_DIM = 128
N_HEADS = 4
GROUP_W = N_HEADS * HEAD_DIM
VMEM_LIMIT = 56 * 1024 * 1024

MLSTM_CHUNK = 128
HGRN_BLOCK = 128
HGRN_SUB = 16
TOKEN_TILE = 512
ROW_TILE = 128
EXPERT_BLOCK = 256
STEP_BATCH = 8


def _cparams(sem, vmem=VMEM_LIMIT):
    return pltpu.CompilerParams(dimension_semantics=sem, vmem_limit_bytes=vmem)


def _dot(a, b):
    return jnp.dot(a, b, preferred_element_type=F32)


def _dot_nt(a, b):
    return lax.dot_general(a, b, (((1,), (1,)), ((), ())), preferred_element_type=F32)


def _dot_tn(a, b):
    return lax.dot_general(a, b, (((0,), (0,)), ((), ())), preferred_element_type=F32)


def _sigmoid(x):
    return 1.0 / (1.0 + jnp.exp(-x))


def _log_sigmoid(x):
    return jnp.minimum(x, 0.0) - jnp.log1p(jnp.exp(-jnp.abs(x)))


def _rms(x, g):
    return x * lax.rsqrt(jnp.mean(x * x, axis=-1, keepdims=True) + EPS) * g


def _cumsum_rows(tri, x):
    hi = x.astype(BF16)
    r1 = x - hi.astype(F32)
    mid = r1.astype(BF16)
    lo = (r1 - mid.astype(F32)).astype(BF16)
    return _dot(tri, hi) + _dot(tri, mid) + _dot(tri, lo)


def _ada_kernel(c_ref, w_ref, b_ref, o_ref):
    c = c_ref[...]
    a = (c * _sigmoid(c)).astype(BF16)
    o_ref[...] = _dot(a, w_ref[...].astype(BF16)) + b_ref[...]


def _ada(c_all, w, b):
    m, d = c_all.shape
    n = w.shape[1]
    tn = 1024
    return pl.pallas_call(
        _ada_kernel,
        out_shape=jax.ShapeDtypeStruct((m, n), F32),
        grid=(n // tn,),
        in_specs=[pl.BlockSpec((m, d), lambda j: (0, 0)),
                  pl.BlockSpec((d, tn), lambda j: (0, j)),
                  pl.BlockSpec((1, tn), lambda j: (0, j))],
        out_specs=pl.BlockSpec((m, tn), lambda j: (0, j)),
        compiler_params=_cparams(("parallel",)),
        name="ada",
    )(c_all, w, b)


def _inproj_kernel(x_ref, sh_ref, sc_ref, nw_ref, w_ref, oa_ref, ob_ref):
    h = _rms(x_ref[...], nw_ref[...]) * (1.0 + sc_ref[...]) + sh_ref[...]
    hb = h.astype(BF16)
    na = oa_ref.shape[1]
    nb = ob_ref.shape[1]
    for j in range(0, na, GROUP_W):
        oa_ref[:, j:j + GROUP_W] = _dot(hb, w_ref[:, j:j + GROUP_W]).astype(oa_ref.dtype)
    ob_ref[:, 0:GROUP_W] = _dot(hb, w_ref[:, na:na + GROUP_W])
    ob_ref[:, GROUP_W:nb] = _dot(hb, w_ref[:, na + GROUP_W:na + nb])


def _inproj(x, mod_rows, mod_is_per_row, norm_w, w_r, n_a, tm, rows_per_batch, out_dtype):
    n, d = x.shape
    n_b = w_r.shape[1] - n_a
    if mod_is_per_row:
        sh_spec = pl.BlockSpec((tm, d), lambda i: (i, 0))
        sc_spec = pl.BlockSpec((tm, d), lambda i: (i, 1))
    else:
        tiles = rows_per_batch // tm
        sh_spec = pl.BlockSpec((None, 1, d), lambda i: (i // tiles, 0, 0))
        sc_spec = pl.BlockSpec((None, 1, d), lambda i: (i // tiles, 0, 1))
    return pl.pallas_call(
        _inproj_kernel,
        out_shape=(jax.ShapeDtypeStruct((n, n_a), out_dtype),
                   jax.ShapeDtypeStruct((n, n_b), F32)),
        grid=(n // tm,),
        in_specs=[pl.BlockSpec((tm, d), lambda i: (i, 0)), sh_spec, sc_spec,
                  pl.BlockSpec((1, d), lambda i: (0, 0)),
                  pl.BlockSpec(w_r.shape, lambda i: (0, 0))],
        out_specs=(pl.BlockSpec((tm, n_a), lambda i: (i, 0)),
                   pl.BlockSpec((tm, n_b), lambda i: (i, 0))),
        compiler_params=_cparams(("parallel",)),
        name="inproj",
    )(x, mod_rows, mod_rows, norm_w, w_r)


def _mlstm_kernel(q_ref, k_ref, v_ref, o_ref, g_ref, bg_ref, na_ref,
                  h_ref, c_ref, n_ref, m_ref):
    L = q_ref.shape[0]
    scale = HEAD_DIM ** -0.5

    @pl.when(pl.program_id(1) == 0)
    def _():
        c_ref[...] = jnp.zeros_like(c_ref)
        n_ref[...] = jnp.zeros_like(n_ref)
        m_ref[...] = jnp.full_like(m_ref, NEG)

    g = g_ref[...] + bg_ref[...]
    lane = lax.broadcasted_iota(I32, (L, LANES), 1)
    gates = jnp.where(lane < N_HEADS, g, _log_sigmoid(g))
    row = lax.broadcasted_iota(I32, (L, L), 0)
    col = lax.broadcasted_iota(I32, (L, L), 1)
    causal = row >= col
    tri = causal.astype(BF16)
    csum = _cumsum_rows(tri, gates)
    gates_t = gates.T
    csum_t = csum.T

    for h in range(N_HEADS):
        sl = slice(h * HEAD_DIM, (h + 1) * HEAD_DIM)
        qh, kh, vh = q_ref[:, sl], k_ref[:, sl], v_ref[:, sl]
        b_col = csum[:, N_HEADS + h:N_HEADS + h + 1]
        li_col = gates[:, h:h + 1]
        b_row = csum_t[N_HEADS + h:N_HEADS + h + 1, :]
        li_row = gates_t[h:h + 1, :]
        m_prev = m_ref[h:h + 1, 0:1]
        c_prev = c_ref[h]
        n_prev = n_ref[h:h + 1, :]

        dm = jnp.where(causal, b_col - b_row + li_row, NEG)
        inter = b_col + m_prev
        m_t = jnp.maximum(inter, jnp.max(dm, axis=-1, keepdims=True))
        w = jnp.exp(dm - m_t) * (_dot_nt(qh, kh) * scale)
        wi = jnp.exp(inter - m_t)
        num = _dot(w.astype(BF16), vh) + wi * _dot(qh, c_prev.astype(BF16))
        den = (jnp.sum(w, axis=-1, keepdims=True)
               + wi * jnp.sum(qh.astype(F32) * n_prev, axis=-1, keepdims=True))
        hval = num / jnp.maximum(jnp.abs(den), jnp.exp(-m_t))

        m_new = m_t[L - 1:L, :]
        b_last = b_col[L - 1:L, :]
        ws = jnp.exp(b_last - b_col + li_col - m_new)
        dec = jnp.exp(b_last + m_prev - m_new)
        kw = kh.astype(F32) * (ws * scale)
        c_ref[h] = dec * c_prev + _dot_tn(kw.astype(BF16), vh)
        n_ref[h:h + 1, :] = dec * n_prev + jnp.sum(kw, axis=0, keepdims=True)
        m_ref[h:h + 1, :] = jnp.broadcast_to(m_new, (1, LANES))

        hn = _rms(hval, na_ref[:, sl]) * _sigmoid(o_ref[:, sl].astype(F32))
        h_ref[:, sl] = hn.astype(h_ref.dtype)


def _mlstm_prompt(pa, pb, b_gate_row, norm_a, bsz, seq):
    L = MLSTM_CHUNK
    nc = seq // L
    n = bsz * seq
    gate_blk = (pb.shape[1] - LANES) // LANES

    def col(j):
        return pl.BlockSpec((L, GROUP_W), lambda b, c: (b * nc + c, j))

    return pl.pallas_call(
        _mlstm_kernel,
        out_shape=(jax.ShapeDtypeStruct((n, GROUP_W), BF16),
                   jax.ShapeDtypeStruct((1, bsz, N_HEADS, HEAD_DIM, HEAD_DIM), F32),
                   jax.ShapeDtypeStruct((1, bsz, N_HEADS, HEAD_DIM), F32),
                   jax.ShapeDtypeStruct((bsz, 8, LANES), F32)),
        grid=(bsz, nc),
        in_specs=[col(0), col(1), col(2), col(3),
                  pl.BlockSpec((L, LANES), lambda b, c: (b * nc + c, gate_blk)),
                  pl.BlockSpec((1, LANES), lambda b, c: (0, 0)),
                  pl.BlockSpec((1, GROUP_W), lambda b, c: (0, 0))],
        out_specs=(pl.BlockSpec((L, GROUP_W), lambda b, c: (b * nc + c, 0)),
                   pl.BlockSpec((None, None, N_HEADS, HEAD_DIM, HEAD_DIM), lambda b, c: (0, b, 0, 0, 0)),
                   pl.BlockSpec((None, None, N_HEADS, HEAD_DIM), lambda b, c: (0, b, 0, 0)),
                   pl.BlockSpec((None, 8, LANES), lambda b, c: (b, 0, 0))),
        compiler_params=_cparams(("parallel", "arbitrary")),
        name="mlstm_prompt",
    )(pa, pa, pa, pa, pb, b_gate_row, norm_a)


def _lower_bound(lb_logits_ref):
    lg = lb_logits_ref[...]
    e = jnp.exp(lg - jnp.max(lg, axis=0, keepdims=True))
    return e[0:1, :] / jnp.sum(e, axis=0, keepdims=True)


def _hgrn_kernel(q_ref, v_ref, g_ref, f_ref, lbl_ref, nb_ref,
                 h_ref, s_ref, st_ref, qs_ref, ks_ref, bs_ref, os_ref):
    LB = q_ref.shape[0]
    C = HGRN_SUB
    scale = HEAD_DIM ** -0.5

    @pl.when(pl.program_id(1) == 0)
    def _():
        st_ref[...] = jnp.zeros_like(st_ref)

    lb = _lower_bound(lbl_ref)
    f = lb + (1.0 - lb) * _sigmoid(f_ref[...])
    qraw = q_ref[...].astype(F32)
    qs_ref[...] = qraw * _sigmoid(qraw) * scale
    ks_ref[...] = 1.0 - f
    row = lax.broadcasted_iota(I32, (LB, LB), 0)
    col = lax.broadcasted_iota(I32, (LB, LB), 1)
    shift = C.bit_length() - 1
    tri = ((row >= col) & ((row >> shift) == (col >> shift))).astype(BF16)
    bs_ref[...] = _cumsum_rows(tri, jnp.log(f))

    r16 = lax.broadcasted_iota(I32, (C, C), 0)
    c16 = lax.broadcasted_iota(I32, (C, C), 1)

    def sub_chunk(i, carry):
        r0 = pl.multiple_of(i * C, C)
        rows = pl.ds(r0, C)
        for h in range(N_HEADS):
            sl = slice(h * HEAD_DIM, (h + 1) * HEAD_DIM)
            q_i = qs_ref[rows, sl]
            k_i = ks_ref[rows, sl]
            b_i = bs_ref[rows, sl]
            v_i = v_ref[rows, sl]
            st = st_ref[h]
            a = jnp.zeros((C, C), F32)
            for s in range(C):
                p = q_i * k_i[s:s + 1, :] * jnp.exp(jnp.minimum(b_i - b_i[s:s + 1, :], 0.0))
                a = jnp.where(c16 == s, jnp.sum(p, axis=-1, keepdims=True), a)
            a = jnp.where(r16 >= c16, a, 0.0)
            o = _dot(a.astype(BF16), v_i) + _dot_nt((q_i * jnp.exp(b_i)).astype(BF16), st.astype(BF16))
            os_ref[rows, sl] = o
            b_l = b_i[C - 1:C, :]
            kd = k_i * jnp.exp(b_l - b_i)
            st_ref[h] = st * jnp.exp(b_l) + _dot_tn(v_i, kd.astype(BF16))
        return carry

    lax.fori_loop(0, LB // C, sub_chunk, 0)

    for h in range(N_HEADS):
        sl = slice(h * HEAD_DIM, (h + 1) * HEAD_DIM)
        gv = g_ref[:, sl].astype(F32)
        hn = _rms(os_ref[:, sl], nb_ref[:, sl]) * (gv * _sigmoid(gv))
        h_ref[:, sl] = hn.astype(h_ref.dtype)

    @pl.when(pl.program_id(1) == pl.num_programs(1) - 1)
    def _():
        for h in range(N_HEADS):
            s_ref[h] = st_ref[h].T


def _hgrn_prompt(pa, pb, lb_logits, norm_b, bsz, seq):
    LB = HGRN_BLOCK
    nc = seq // LB
    n = bsz * seq

    def col(j):
        return pl.BlockSpec((LB, GROUP_W), lambda b, c: (b * nc + c, j))

    return pl.pallas_call(
        _hgrn_kernel,
        out_shape=(jax.ShapeDtypeStruct((n, GROUP_W), BF16),
                   jax.ShapeDtypeStruct((1, bsz, N_HEADS, HEAD_DIM, HEAD_DIM), F32)),
        grid=(bsz, nc),
        in_specs=[col(4), col(5), col(6),
                  pl.BlockSpec((LB, GROUP_W), lambda b, c: (b * nc + c, 0)),
                  pl.BlockSpec(lb_logits.shape, lambda b, c: (0, 0)),
                  pl.BlockSpec((1, GROUP_W), lambda b, c: (0, 0))],
        out_specs=(pl.BlockSpec((LB, GROUP_W), lambda b, c: (b * nc + c, 0)),
                   pl.BlockSpec((None, None, N_HEADS, HEAD_DIM, HEAD_DIM), lambda b, c: (0, b, 0, 0, 0))),
        scratch_shapes=[pltpu.VMEM((N_HEADS, HEAD_DIM, HEAD_DIM), F32)]
                       + [pltpu.VMEM((LB, GROUP_W), F32)] * 4,
        compiler_params=_cparams(("parallel", "arbitrary")),
        name="hgrn_prompt",
    )(pa, pa, pa, pb, lb_logits, norm_b)


def _step_kernel(pa_ref, pb_ref, bg_ref, na_ref, nb_ref, lbl_ref,
                 c0_ref, n0_ref, m0_ref, s0_ref,
                 ha_ref, hb_ref, c1_ref, n1_ref, m1_ref, s1_ref, ta_ref, tb_ref):
    scale = HEAD_DIM ** -0.5
    W = GROUP_W
    H = N_HEADS
    D = HEAD_DIM
    lb = _lower_bound(lbl_ref)
    gates_all = pb_ref[:, W:W + LANES] + bg_ref[...]
    f_all = lb + (1.0 - lb) * _sigmoid(pb_ref[:, 0:W])
    ta_ref[...] = jnp.zeros_like(ta_ref)
    tb_ref[...] = jnp.zeros_like(tb_ref)

    aux_a, aux_b = {}, {}
    for j in range(STEP_BATCH):
        row = slice(j, j + 1)
        for h in range(H):
            q = pa_ref[row, h * D:(h + 1) * D]
            k = pa_ref[row, W + h * D:W + (h + 1) * D]
            li = gates_all[row, h:h + 1]
            lf = _log_sigmoid(gates_all[row, H + h:H + h + 1])
            inter = lf + m0_ref[row, h:h + 1]
            m_t = jnp.maximum(inter, li)
            ws = jnp.exp(li - m_t)
            dec = jnp.exp(inter - m_t)
            kw = k * (ws * scale)
            ta_ref[2 * H * j + h:2 * H * j + h + 1, :] = q
            ta_ref[2 * H * j + H + h:2 * H * j + H + h + 1, :] = kw
            aux_a[j, h] = (q, k, kw, m_t, ws, dec)

            qraw = pa_ref[row, 4 * W + h * D:4 * W + (h + 1) * D]
            qb = qraw * _sigmoid(qraw) * scale
            f = f_all[row, h * D:(h + 1) * D]
            decay = jnp.exp(jnp.log(f))
            kb = 1.0 - f
            tb_ref[3 * H * j + h:3 * H * j + h + 1, :] = qb * decay
            tb_ref[3 * H * j + H + h:3 * H * j + H + h + 1, :] = decay
            tb_ref[3 * H * j + 2 * H + h:3 * H * j + 2 * H + h + 1, :] = kb
            aux_b[j, h] = (qb, kb)

    ta = ta_ref[...].T
    tb = tb_ref[...].T
    for j in range(STEP_BATCH):
        row = slice(j, j + 1)
        for h in range(H):
            sl = slice(h * D, (h + 1) * D)
            q, k, kw, m_t, ws, dec = aux_a[j, h]
            v = pa_ref[row, 2 * W + h * D:2 * W + (h + 1) * D]
            og = pa_ref[row, 3 * W + h * D:3 * W + (h + 1) * D]
            q_col = ta[:, 2 * H * j + h:2 * H * j + h + 1]
            kw_col = ta[:, 2 * H * j + H + h:2 * H * j + H + h + 1]
            c0 = c0_ref[j, h]
            n0 = n0_ref[j, h:h + 1, :]
            w = ws * (jnp.sum(q * k, axis=-1, keepdims=True) * scale)
            num = w * v + dec * jnp.sum(q_col * c0, axis=0, keepdims=True)
            den = w + dec * jnp.sum(q * n0, axis=-1, keepdims=True)
            hval = num / jnp.maximum(jnp.abs(den), jnp.exp(-m_t))
            c1_ref[j, h] = dec * c0 + kw_col * v
            n1_ref[j, h:h + 1, :] = dec * n0 + kw
            m1_ref[row, h:h + 1] = m_t
            ha_ref[row, sl] = _rms(hval, na_ref[:, sl]) * _sigmoid(og)
            qb, kb = aux_b[j, h]
            vb = pa_ref[row, 5 * W + h * D:5 * W + (h + 1) * D]
            gv = pa_ref[row, 6 * W + h * D:6 * W + (h + 1) * D]
            qd_col = tb[:, 3 * H * j + h:3 * H * j + h + 1]
            d_col = tb[:, 3 * H * j + H + h:3 * H * j + H + h + 1]
            k_col = tb[:, 3 * H * j + 2 * H + h:3 * H * j + 2 * H + h + 1]
            s0 = s0_ref[j, h]
            a = jnp.sum(qb * kb, axis=-1, keepdims=True)
            o = a * vb + jnp.sum(qd_col * s0, axis=0, keepdims=True)
            s1_ref[j, h] = d_col * s0 + k_col * vb
            hb_ref[row, sl] = _rms(o, nb_ref[:, sl]) * (gv * _sigmoid(gv))


def _step_mixers(pa, pb, b_gate_row, norm_a, norm_b, lb_logits, c0, n0, m0, s0):
    bs = pa.shape[0]
    sb = STEP_BATCH
    st5 = pl.BlockSpec((None, sb, N_HEADS, HEAD_DIM, HEAD_DIM), lambda i: (0, i, 0, 0, 0))
    st4 = pl.BlockSpec((None, sb, N_HEADS, HEAD_DIM), lambda i: (0, i, 0, 0))
    st3 = pl.BlockSpec((None, sb, N_HEADS), lambda i: (0, i, 0))
    rowblk = lambda w: pl.BlockSpec((sb, w), lambda i: (i, 0))
    const = lambda a: pl.BlockSpec(a.shape, lambda i: (0,) * a.ndim)
    return pl.pallas_call(
        _step_kernel,
        out_shape=(jax.ShapeDtypeStruct((bs, GROUP_W), F32),
                   jax.ShapeDtypeStruct((bs, GROUP_W), F32),
                   jax.ShapeDtypeStruct(c0.shape, F32),
                   jax.ShapeDtypeStruct(n0.shape, F32),
                   jax.ShapeDtypeStruct(m0.shape, F32),
                   jax.ShapeDtypeStruct(s0.shape, F32)),
        grid=(bs // sb,),
        in_specs=[rowblk(pa.shape[1]), rowblk(pb.shape[1]), const(b_gate_row), const(norm_a),
                  const(norm_b), const(lb_logits), st5, st4, st3, st5],
        out_specs=(rowblk(GROUP_W), rowblk(GROUP_W), st5, st4, st3, st5),
        scratch_shapes=[pltpu.VMEM((LANES, LANES), F32), pltpu.VMEM((LANES, LANES), F32)],
        compiler_params=_cparams(("parallel",)),
        name="step_mixers",
    )(pa, pb, b_gate_row, norm_a, norm_b, lb_logits, c0, n0, m0, s0)


def _post_rows(ha_ref, hb_ref, x_ref, g1_ref, sh2_ref, sc2_ref, nf_ref, wo_ref, rw_ref, rb_ref,
               x1_ref, h2_ref, idx_ref, gate_ref):
    tm = x_ref.shape[0]
    mix = (_dot(ha_ref[...].astype(BF16), wo_ref[0:GROUP_W, :])
           + _dot(hb_ref[...].astype(BF16), wo_ref[GROUP_W:2 * GROUP_W, :]))
    x1 = x_ref[...] + g1_ref[...] * mix
    x1_ref[0:tm, :] = x1
    h2 = _rms(x1, nf_ref[...]) * (1.0 + sc2_ref[...]) + sh2_ref[...]
    h2_ref[0:tm, :] = h2
    logits = _dot(h2.astype(BF16), rw_ref[...]) + rb_ref[...]
    lane = lax.broadcasted_iota(I32, (tm, LANES), 1)
    lane_f = lane.astype(F32)
    cur = logits
    vals, ids = [], []
    for _ in range(TOP_K):
        mx = jnp.max(cur, axis=-1, keepdims=True)
        am = jnp.min(jnp.where(cur == mx, lane_f, float(LANES)), axis=-1, keepdims=True)
        vals.append(mx)
        ids.append(am)
        cur = jnp.where(lane_f == am, -jnp.inf, cur)
    es = [jnp.exp(v - vals[0]) for v in vals]
    tot = es[0] + es[1] + es[2] + es[3]
    idx_out = jnp.full((tm, LANES), -1.0, F32)
    gate_out = jnp.zeros((tm, LANES), F32)
    for k in range(TOP_K):
        idx_out = jnp.where(lane == k, ids[k], idx_out)
        gate_out = jnp.where(lane == k, es[k] / tot, gate_out)
    idx_ref[0:tm, :] = idx_out.astype(I32)
    gate_ref[0:tm, :] = gate_out


def _post_kernel(ha_ref, hb_ref, x_ref, g1_ref, sh2_ref, sc2_ref,
                 has_ref, hbs_ref, xs_ref, g1s_ref, sh2s_ref, sc2s_ref,
                 nf_ref, wo_ref, rw_ref, rb_ref, x1_ref, h2_ref, idx_ref, gate_ref, *, n_prompt_tiles):
    i = pl.program_id(0)
    shared = (nf_ref, wo_ref, rw_ref, rb_ref, x1_ref, h2_ref, idx_ref, gate_ref)

    @pl.when(i < n_prompt_tiles)
    def _():
        _post_rows(ha_ref, hb_ref, x_ref, g1_ref, sh2_ref, sc2_ref, *shared)

    @pl.when(i == n_prompt_tiles)
    def _():
        _post_rows(has_ref, hbs_ref, xs_ref, g1s_ref, sh2s_ref, sc2s_ref, *shared)


def _post(ha_p, hb_p, x_p, mod_p, ha_s, hb_s, x_s, mod_s, norm_ffn, w_out, rw, rb, tm, rows_per_batch):
    n_p, d = x_p.shape
    n_s = x_s.shape[0]
    assert n_s <= tm
    npt = n_p // tm
    tiles = rows_per_batch // tm
    nbatch = n_p // rows_per_batch
    n_all = n_p + n_s
    prow = lambda w: pl.BlockSpec((tm, w), lambda i: (jnp.minimum(i, npt - 1), 0))
    pmod = lambda j: pl.BlockSpec((None, 1, d), lambda i: (jnp.minimum(i // tiles, nbatch - 1), 0, j))
    smod = lambda j: pl.BlockSpec((n_s, d), lambda i: (0, j))
    const = lambda a: pl.BlockSpec(a.shape, lambda i: (0,) * a.ndim)
    out_blk = lambda w: pl.BlockSpec((tm, w), lambda i: (i, 0))
    return pl.pallas_call(
        functools.partial(_post_kernel, n_prompt_tiles=npt),
        out_shape=(jax.ShapeDtypeStruct((n_all, d), F32),
                   jax.ShapeDtypeStruct((n_all, d), F32),
                   jax.ShapeDtypeStruct((n_all, LANES), I32),
                   jax.ShapeDtypeStruct((n_all, LANES), F32)),
        grid=(npt + 1,),
        in_specs=[prow(GROUP_W), prow(GROUP_W), prow(d), pmod(2), pmod(3), pmod(4),
                  const(ha_s), const(hb_s), const(x_s), smod(2), smod(3), smod(4),
                  const(norm_ffn), const(w_out), const(rw), const(rb)],
        out_specs=(out_blk(d), out_blk(d), out_blk(LANES), out_blk(LANES)),
        compiler_params=_cparams(("parallel",)),
        name="post",
    )(ha_p, hb_p, x_p, mod_p, mod_p, mod_p, ha_s, hb_s, x_s, mod_s, mod_s, mod_s, norm_ffn, w_out, rw, rb)


def _route_tile(n):
    return max(r for r in range(LANES, 4 * LANES + 1, LANES) if n % r == 0)


def _rank_kernel(idx_ref, rank_ref, cnt_ref, base_ref):
    R = idx_ref.shape[0]

    @pl.when(pl.program_id(0) == 0)
    def _():
        base_ref[...] = jnp.zeros_like(base_ref)

    idx = idx_ref[...]
    lane = lax.broadcasted_iota(I32, (R, LANES), 1)
    hits = [lane == idx[:, k:k + 1] for k in range(TOP_K)]
    onehot = jnp.zeros((R, LANES), F32)
    for hk in hits:
        onehot = onehot + hk.astype(F32)
    row = lax.broadcasted_iota(I32, (R, R), 0)
    col = lax.broadcasted_iota(I32, (R, R), 1)
    before = _dot((row > col).astype(BF16), onehot.astype(BF16)) + base_ref[...]
    rank = jnp.zeros((R, LANES), F32)
    for k, hk in enumerate(hits):
        rk = jnp.sum(jnp.where(hk, before, 0.0), axis=-1, keepdims=True)
        rank = jnp.where(lane == k, rk, rank)
    rank_ref[...] = rank
    base = base_ref[...] + jnp.sum(onehot, axis=0, keepdims=True)
    base_ref[...] = base
    cnt_ref[...] = base


def _rank(idx):
    n = idx.shape[0]
    R = _route_tile(n)
    return pl.pallas_call(
        _rank_kernel,
        out_shape=(jax.ShapeDtypeStruct((n, LANES), F32), jax.ShapeDtypeStruct((1, LANES), F32)),
        grid=(n // R,),
        in_specs=[pl.BlockSpec((R, LANES), lambda i: (i, 0))],
        out_specs=(pl.BlockSpec((R, LANES), lambda i: (i, 0)), pl.BlockSpec((1, LANES), lambda i: (0, 0))),
        scratch_shapes=[pltpu.VMEM((1, LANES), F32)],
        compiler_params=_cparams(("arbitrary",)),
        name="rank",
    )(idx)


def _dest_kernel(idx_ref, rank_ref, start_ref, dest_ref):
    R = idx_ref.shape[0]
    idx = idx_ref[...]
    lane = lax.broadcasted_iota(I32, (R, LANES), 1)
    dest = rank_ref[...]
    for k in range(TOP_K):
        st = jnp.sum(jnp.where(lane == idx[:, k:k + 1], start_ref[...], 0.0), axis=-1, keepdims=True)
        dest = dest + jnp.where(lane == k, st, 0.0)
    dest_ref[...] = dest.T[0:8, :].astype(I32)


def _dest(idx, rank, start_row):
    n = idx.shape[0]
    R = _route_tile(n)
    return pl.pallas_call(
        _dest_kernel,
        out_shape=jax.ShapeDtypeStruct((8, n), I32),
        grid=(n // R,),
        in_specs=[pl.BlockSpec((R, LANES), lambda i: (i, 0)),
                  pl.BlockSpec((R, LANES), lambda i: (i, 0)),
                  pl.BlockSpec((1, LANES), lambda i: (0, 0))],
        out_specs=pl.BlockSpec((8, R), lambda i: (0, i)),
        compiler_params=_cparams(("parallel",)),
        name="dest",
    )(idx, rank, start_row)


def _dispatch_kernel(zb_ref, dest_hbm, h2_hbm, xs_hbm, dest_smem, zeros, sem_idx, sem_rows, sem_zero):
    R = dest_smem.shape[2]
    bm = zeros.shape[0]
    i = pl.program_id(0)
    last = pl.num_programs(0) - 1
    slot = lax.rem(i, 3)
    prev_slot = lax.rem(i + 2, 3)
    next_slot = lax.rem(i + 1, 3)

    def dest_load(tile, s):
        return pltpu.make_async_copy(dest_hbm.at[:, pl.ds(pl.multiple_of(tile * R, R), R)],
                                     dest_smem.at[s], sem_idx.at[s])

    @pl.when(i == 0)
    def _():
        zeros[...] = jnp.zeros_like(zeros)

        def zero_copy(j):
            return pltpu.make_async_copy(zeros, xs_hbm.at[pl.ds(pl.multiple_of(zb_ref[j] * bm, bm), bm)], sem_zero)

        def start(j, c):
            @pl.when(zb_ref[j] >= 0)
            def _():
                zero_copy(j).start()
            return c

        def wait(j, c):
            @pl.when(zb_ref[j] >= 0)
            def _():
                zero_copy(j).wait()
            return c

        lax.fori_loop(0, zb_ref.shape[0], start, 0)
        lax.fori_loop(0, zb_ref.shape[0], wait, 0)
        dest_load(0, 0).start()

    dest_load(i, slot).wait()

    @pl.when(i < last)
    def _():
        dest_load(i + 1, next_slot).start()

    def row_copies(tile, s, t):
        return [pltpu.make_async_copy(h2_hbm.at[pl.ds(tile * R + t, 1)],
                                      xs_hbm.at[pl.ds(dest_smem[s, k, t], 1)], sem_rows.at[lax.rem(tile, 2)])
                for k in range(TOP_K)]

    def issue(t, c):
        for cp in row_copies(i, slot, t):
            cp.start()
        return c

    def drain(tile, s):
        def wait(t, c):
            for cp in row_copies(tile, s, t):
                cp.wait()
            return c
        lax.fori_loop(0, R, wait, 0)

    lax.fori_loop(0, R, issue, 0)

    @pl.when(i > 0)
    def _():
        drain(i - 1, prev_slot)

    @pl.when(i == last)
    def _():
        drain(i, slot)


def _dispatch(zero_blocks, dest, h2, p_rows):
    n, d = h2.shape
    R = ROW_TILE
    return pl.pallas_call(
        _dispatch_kernel,
        out_shape=jax.ShapeDtypeStruct((p_rows, d), F32),
        grid_spec=pltpu.PrefetchScalarGridSpec(
            num_scalar_prefetch=1,
            grid=(n // R,),
            in_specs=[pl.BlockSpec(memory_space=pl.ANY),
                      pl.BlockSpec(memory_space=pl.ANY)],
            out_specs=pl.BlockSpec(memory_space=pl.ANY),
            scratch_shapes=[pltpu.SMEM((3, 8, R), I32), pltpu.VMEM((EXPERT_BLOCK, d), F32),
                            pltpu.SemaphoreType.DMA((3,)), pltpu.SemaphoreType.DMA((2,)), pltpu.SemaphoreType.DMA]),
        compiler_params=_cparams(("arbitrary",)),
        name="dispatch",
    )(zero_blocks, dest, h2)


def _expert_kernel(be_ref, nv_ref, x_ref, wgu_ref, bgu_ref, wd_ref, bd_ref, y_ref, wgu_bf, wd_bf):
    i = pl.program_id(0)
    dff = wd_ref.shape[0]

    @pl.when(i < nv_ref[0])
    def _():
        changed = jnp.logical_or(i == 0, be_ref[i] != be_ref[jnp.maximum(i - 1, 0)])

        @pl.when(changed)
        def _():
            rows = 128

            def cast_gu(r, c):
                sl = pl.ds(pl.multiple_of(r * rows, rows), rows)
                wgu_bf[sl, :] = wgu_ref[sl, :].astype(BF16)
                return c

            def cast_d(r, c):
                sl = pl.ds(pl.multiple_of(r * rows, rows), rows)
                wd_bf[sl, :] = wd_ref[sl, :].astype(BF16)
                return c

            lax.fori_loop(0, wgu_ref.shape[0] // rows, cast_gu, 0)
            lax.fori_loop(0, wd_ref.shape[0] // rows, cast_d, 0)

        x = x_ref[...].astype(BF16)
        g = jnp.minimum(_dot(x, wgu_bf[:, 0:dff]) + bgu_ref[:, 0:dff], SWIGLU_LIMIT)
        u = jnp.clip(_dot(x, wgu_bf[:, dff:2 * dff]) + bgu_ref[:, dff:2 * dff], -SWIGLU_LIMIT, SWIGLU_LIMIT)
        act = (u + 1.0) * (g * _sigmoid(SWIGLU_ALPHA * g))
        y_ref[...] = _dot(act.astype(BF16), wd_bf[...]) + bd_ref[...]

    @pl.when(i >= nv_ref[0])
    def _():
        y_ref[...] = jnp.zeros_like(y_ref)


def _experts(block_e, n_valid, xs, w_gu, b_gu, w_down, b_down):
    p_rows, d = xs.shape
    bm = EXPERT_BLOCK
    nb = p_rows // bm
    e, _, dff2 = w_gu.shape[1:]
    dff = w_down.shape[2]
    blk = lambda i, be, nv: (jnp.minimum(i, nv[0] - 1), 0)
    return pl.pallas_call(
        _expert_kernel,
        out_shape=jax.ShapeDtypeStruct((p_rows, d), F32),
        grid_spec=pltpu.PrefetchScalarGridSpec(
            num_scalar_prefetch=2,
            grid=(nb,),
            in_specs=[pl.BlockSpec((bm, d), blk),
                      pl.BlockSpec((None, None, d, dff2), lambda i, be, nv: (0, be[i], 0, 0)),
                      pl.BlockSpec((None, 1, dff2), lambda i, be, nv: (be[i], 0, 0)),
                      pl.BlockSpec((None, None, dff, d), lambda i, be, nv: (0, be[i], 0, 0)),
                      pl.BlockSpec((None, 1, d), lambda i, be, nv: (be[i], 0, 0))],
            out_specs=pl.BlockSpec((bm, d), lambda i, be, nv: (i, 0)),
            scratch_shapes=[pltpu.VMEM((d, dff2), BF16), pltpu.VMEM((dff, d), BF16)]),
        compiler_params=_cparams(("arbitrary",)),
        name="experts",
    )(block_e, n_valid, xs, w_gu, b_gu.reshape(e, 1, dff2), w_down, b_down.reshape(e, 1, d))


def _combine_kernel(dest_hbm, ys_hbm, x1_ref, gate_ref, g2p_ref, g2s_ref, nf_ref,
                    yp_ref, ysm_ref, dest_smem, buf, sem_idx, sem_rows, *, n_prompt_tiles):
    R = x1_ref.shape[0]
    i = pl.program_id(0)
    last = pl.num_programs(0) - 1
    slot = lax.rem(i, 2)

    def dest_load(tile, s):
        return pltpu.make_async_copy(dest_hbm.at[:, pl.ds(pl.multiple_of(tile * R, R), R)],
                                     dest_smem.at[s], sem_idx.at[s])

    def row_copy(s, src_row, k, t):
        return pltpu.make_async_copy(ys_hbm.at[pl.ds(src_row, 1)], buf.at[s, k, pl.ds(t, 1)], sem_rows.at[s])

    def gather(s):
        def issue(t, c):
            for k in range(TOP_K):
                row_copy(s, dest_smem[s, k, t], k, t).start()
            return c
        lax.fori_loop(0, R, issue, 0)

    @pl.when(i == 0)
    def _():
        dest_load(0, 0).start()
        dest_load(0, 0).wait()

        @pl.when(last > 0)
        def _():
            dest_load(1, 1).start()
        gather(0)

    def drain(t, c):
        for k in range(TOP_K):
            row_copy(slot, dest_smem[slot, k, t], k, t).wait()
        return c

    lax.fori_loop(0, R, drain, 0)

    @pl.when(i < last)
    def _():
        dest_load(i + 1, 1 - slot).wait()
        gather(1 - slot)

        @pl.when(i + 2 <= last)
        def _():
            dest_load(i + 2, slot).start()

    gate = gate_ref[...]
    ff = jnp.zeros(x1_ref.shape, F32)
    for k in range(TOP_K):
        ff = ff + buf[slot, k] * gate[:, k:k + 1]
    is_prompt = i < n_prompt_tiles

    @pl.when(is_prompt)
    def _():
        yp_ref[...] = _rms(x1_ref[...] + g2p_ref[...] * ff, nf_ref[...])

    @pl.when(jnp.logical_not(is_prompt))
    def _():
        ysm_ref[...] = _rms(x1_ref[...] + g2s_ref[...] * ff, nf_ref[...])


def _combine(dest, ys, x1, gates, mod_p, mod_s, norm_final, n_prompt, rows_per_batch):
    n, d = x1.shape
    R = ROW_TILE
    npt = n_prompt // R
    tiles = rows_per_batch // R
    nbatch = n_prompt // rows_per_batch
    n_s = n - n_prompt
    return pl.pallas_call(
        functools.partial(_combine_kernel, n_prompt_tiles=npt),
        out_shape=(jax.ShapeDtypeStruct((n_prompt, d), F32), jax.ShapeDtypeStruct((n_s, d), F32)),
        grid=(n // R,),
        in_specs=[pl.BlockSpec(memory_space=pl.ANY),
                  pl.BlockSpec(memory_space=pl.ANY),
                  pl.BlockSpec((R, d), lambda i: (i, 0)),
                  pl.BlockSpec((R, LANES), lambda i: (i, 0)),
                  pl.BlockSpec((None, 1, d), lambda i: (jnp.minimum(i // tiles, nbatch - 1), 0, 5)),
                  pl.BlockSpec((R, d), lambda i: (jnp.maximum(i - npt, 0), 5)),
                  pl.BlockSpec((1, d), lambda i: (0, 0))],
        out_specs=(pl.BlockSpec((R, d), lambda i: (jnp.minimum(i, npt - 1), 0)),
                   pl.BlockSpec((R, d), lambda i: (jnp.maximum(i - npt, 0), 0))),
        scratch_shapes=[pltpu.SMEM((2, 8, R), I32), pltpu.VMEM((2, TOP_K, R, d), F32),
                        pltpu.SemaphoreType.DMA((2,)), pltpu.SemaphoreType.DMA((2,))],
        compiler_params=_cparams(("arbitrary",)),
        name="combine",
    )(dest, ys, x1, gates, mod_p, mod_s, norm_final)


def _reorder_w_in(w):
    W = GROUP_W
    g0 = 4 * W
    g1 = g0 + 2 * N_HEADS
    pad = jnp.zeros((w.shape[0], LANES - 2 * N_HEADS), w.dtype)
    cols = [w[:, 0:g0], w[:, g1:g1 + W], w[:, g1 + 2 * W:g1 + 4 * W], w[:, g1 + W:g1 + 2 * W], w[:, g0:g1], pad]
    return jnp.concatenate(cols, axis=1).astype(BF16)


def kernel(x_prompt, x_sample, c_prompt, c_sample, state_mlstm_C, state_mlstm_n, state_mlstm_m,
           state_hgrn_S, w_ada, b_ada, norm_mix, norm_ffn, w_in, b_gate, norm_a, lb_logits, norm_b,
           w_out, router_w, router_b, w_gu, b_gu, w_down, b_down, norm_final):
    bp, seq, d = x_prompt.shape
    bs = x_sample.shape[0]
    assert x_sample.shape[1] == 1 and w_ada.shape[0] == 1
    n_p = bp * seq
    n_all = n_p + bs
    n_exp = router_w.shape[2]
    W = GROUP_W
    n_a = 7 * W

    mod = _ada(jnp.concatenate([c_prompt, c_sample], axis=0), w_ada[0], b_ada)
    mod_p = mod[:bp].reshape(bp, 1, 6 * d)
    mod_s = mod[bp:]

    w_r = _reorder_w_in(w_in[0])
    nmix = norm_mix.reshape(1, d)
    tm = min(TOKEN_TILE, seq)
    xp = x_prompt.reshape(n_p, d)
    xs_ = x_sample.reshape(bs, d)
    pa_p, pb_p = _inproj(xp, mod_p, False, nmix, w_r, n_a, tm, seq, BF16)
    pa_s, pb_s = _inproj(xs_, mod_s, True, nmix, w_r, n_a, bs, None, F32)

    bg_row = jnp.pad(b_gate.reshape(1, 2 * N_HEADS), ((0, 0), (0, LANES - 2 * N_HEADS)))
    na = norm_a.reshape(1, W)
    nb_ = norm_b.reshape(1, W)
    ha_p, c_p, nrm_p, m_p = _mlstm_prompt(pa_p, pb_p, bg_row, na, bp, seq)
    hb_p, s_p = _hgrn_prompt(pa_p, pb_p, lb_logits, nb_, bp, seq)
    ha_s, hb_s, c_s, nrm_s, m_s, s_s = _step_mixers(
        pa_s, pb_s, bg_row, na, nb_, lb_logits, state_mlstm_C, state_mlstm_n, state_mlstm_m, state_hgrn_S)

    w_o = w_out[0].astype(BF16)
    rw = jnp.pad(router_w[0], ((0, 0), (0, LANES - n_exp))).astype(BF16)
    rb = jnp.pad(router_b.reshape(1, n_exp), ((0, 0), (0, LANES - n_exp)), constant_values=NEG)
    nffn = norm_ffn.reshape(1, d)
    x1, h2, idx, gates = _post(ha_p, hb_p, xp, mod_p, ha_s, hb_s, xs_, mod_s, nffn, w_o, rw, rb, tm, seq)

    bm = EXPERT_BLOCK
    rank, counts = _rank(idx)
    cnt = counts[0].astype(I32)
    padded = ((cnt + bm - 1) // bm) * bm
    ends = jnp.cumsum(padded)
    start_row = (ends - padded).astype(F32).reshape(1, LANES)
    dest = _dest(idx, rank, start_row)
    n_blocks = (n_all * TOP_K + n_exp * (bm - 1) + bm - 1) // bm
    n_valid = (ends[n_exp - 1] // bm).astype(I32)
    blk_start = jnp.arange(n_blocks, dtype=I32) * bm
    block_e = jnp.sum((ends[None, :n_exp] <= blk_start[:, None]).astype(I32), axis=1)
    block_e = jnp.minimum(block_e, n_exp - 1)
    last_e = block_e[jnp.maximum(n_valid - 1, 0)]
    block_e = jnp.where(jnp.arange(n_blocks) < n_valid, block_e, last_e)

    min_valid = -(-(n_all * TOP_K) // bm)
    group_last = jnp.where(padded[:n_exp] > 0, ends[:n_exp] // bm - 1, -1)
    tail = n_valid + jnp.arange(n_blocks - min_valid, dtype=I32)
    tail = jnp.where(tail < n_blocks, tail, -1)
    zero_blocks = jnp.concatenate([group_last.astype(I32), tail])

    xs_sorted = _dispatch(zero_blocks, dest, h2, n_blocks * bm)
    ys_sorted = _experts(block_e, n_valid.reshape(1), xs_sorted, w_gu, b_gu[0], w_down, b_down[0])
    y_p, y_s = _combine(dest, ys_sorted, x1, gates, mod_p, mod_s, norm_final.reshape(1, d), n_p, seq)

    m_p_out = m_p[:, :N_HEADS, 0][None]
    return (y_p.reshape(bp, seq, d), y_s.reshape(bs, 1, d), c_p, nrm_p, m_p_out, s_p,
            c_s, nrm_s, m_s, s_s)
```

```python
import functools

import jax
import jax.numpy as jnp
from jax import lax
from jax.experimental import pallas as pl
from jax.experimental.pallas import tpu as pltpu

F32 = jnp.float32
BF16 = jnp.bfloat16
I32 = jnp.int32

EPS = 1e-6
NEG = -1e30
SWIGLU_LIMIT = 7.0
SWIGLU_ALPHA = 1.702
TOP_K = 4

LANES = 128
HEAD_DIM = 128
N_HEADS = 4
GROUP_W = N_HEADS * HEAD_DIM
VMEM_LIMIT = 56 * 1024 * 1024

MLSTM_CHUNK = 128
HGRN_BLOCK = 128
HGRN_SUB = 16
TOKEN_TILE = 512
ROW_TILE = 128
EXPERT_BLOCK = 256
STEP_BATCH = 8
DISPATCH_GROUP = 64


def _cparams(sem, vmem=VMEM_LIMIT):
    return pltpu.CompilerParams(dimension_semantics=sem, vmem_limit_bytes=vmem)


def _dot(a, b):
    return jnp.dot(a, b, preferred_element_type=F32)


def _dot_nt(a, b):
    return lax.dot_general(a, b, (((1,), (1,)), ((), ())), preferred_element_type=F32)


def _dot_tn(a, b):
    return lax.dot_general(a, b, (((0,), (0,)), ((), ())), preferred_element_type=F32)


def _sigmoid(x):
    return 1.0 / (1.0 + jnp.exp(-x))


def _log_sigmoid(x):
    return jnp.minimum(x, 0.0) - jnp.log1p(jnp.exp(-jnp.abs(x)))


def _rms(x, g):
    return x * lax.rsqrt(jnp.mean(x * x, axis=-1, keepdims=True) + EPS) * g


def _cumsum_rows(tri, x):
    hi = x.astype(BF16)
    r1 = x - hi.astype(F32)
    mid = r1.astype(BF16)
    lo = (r1 - mid.astype(F32)).astype(BF16)
    return _dot(tri, hi) + _dot(tri, mid) + _dot(tri, lo)


SUBLANES = 8


def _store_row_tiles(ref, x):
    rows = x.shape[0]
    for c in range(SUBLANES):
        ref[pl.ds(c, rows, stride=SUBLANES), :] = x[:, c * LANES:(c + 1) * LANES]


def _load_row_tiles(ref, rows, lead=()):
    return jnp.concatenate([ref[lead + (pl.ds(c, rows, stride=SUBLANES), slice(None))]
                            for c in range(SUBLANES)], axis=-1)


def _ada_kernel(c_ref, w_ref, b_ref, o_ref):
    c = c_ref[...]
    a = (c * _sigmoid(c)).astype(BF16)
    o_ref[...] = _dot(a, w_ref[...].astype(BF16)) + b_ref[...]


def _ada(c_all, w, b):
    m, d = c_all.shape
    n = w.shape[1]
    tn = 1024
    return pl.pallas_call(
        _ada_kernel,
        out_shape=jax.ShapeDtypeStruct((m, n), F32),
        grid=(n // tn,),
        in_specs=[pl.BlockSpec((m, d), lambda j: (0, 0)),
                  pl.BlockSpec((d, tn), lambda j: (0, j)),
                  pl.BlockSpec((1, tn), lambda j: (0, j))],
        out_specs=pl.BlockSpec((m, tn), lambda j: (0, j)),
        compiler_params=_cparams(("parallel",)),
        name="ada",
    )(c_all, w, b)


def _inproj_kernel(x_ref, sh_ref, sc_ref, nw_ref, w_ref, oa_ref, ob_ref):
    h = _rms(x_ref[...], nw_ref[...]) * (1.0 + sc_ref[...]) + sh_ref[...]
    hb = h.astype(BF16)
    na = oa_ref.shape[1]
    nb = ob_ref.shape[1]
    for j in range(0, na, GROUP_W):
        oa_ref[:, j:j + GROUP_W] = _dot(hb, w_ref[:, j:j + GROUP_W]).astype(oa_ref.dtype)
    ob_ref[:, 0:GROUP_W] = _dot(hb, w_ref[:, na:na + GROUP_W])
    ob_ref[:, GROUP_W:nb] = _dot(hb, w_ref[:, na + GROUP_W:na + nb])


def _inproj(x, mod_rows, mod_is_per_row, norm_w, w_r, n_a, tm, rows_per_batch, out_dtype):
    n, d = x.shape
    n_b = w_r.shape[1] - n_a
    if mod_is_per_row:
        sh_spec = pl.BlockSpec((tm, d), lambda i: (i, 0))
        sc_spec = pl.BlockSpec((tm, d), lambda i: (i, 1))
    else:
        tiles = rows_per_batch // tm
        sh_spec = pl.BlockSpec((None, 1, d), lambda i: (i // tiles, 0, 0))
        sc_spec = pl.BlockSpec((None, 1, d), lambda i: (i // tiles, 0, 1))
    return pl.pallas_call(
        _inproj_kernel,
        out_shape=(jax.ShapeDtypeStruct((n, n_a), out_dtype),
                   jax.ShapeDtypeStruct((n, n_b), F32)),
        grid=(n // tm,),
        in_specs=[pl.BlockSpec((tm, d), lambda i: (i, 0)), sh_spec, sc_spec,
                  pl.BlockSpec((1, d), lambda i: (0, 0)),
                  pl.BlockSpec(w_r.shape, lambda i: (0, 0))],
        out_specs=(pl.BlockSpec((tm, n_a), lambda i: (i, 0)),
                   pl.BlockSpec((tm, n_b), lambda i: (i, 0))),
        compiler_params=_cparams(("parallel",)),
        name="inproj",
    )(x, mod_rows, mod_rows, norm_w, w_r)


def _mlstm_kernel(q_ref, k_ref, v_ref, o_ref, g_ref, bg_ref, na_ref,
                  h_ref, c_ref, n_ref, m_ref):
    L = q_ref.shape[0]
    scale = HEAD_DIM ** -0.5

    @pl.when(pl.program_id(1) == 0)
    def _():
        c_ref[...] = jnp.zeros_like(c_ref)
        n_ref[...] = jnp.zeros_like(n_ref)
        m_ref[...] = jnp.full_like(m_ref, NEG)

    g = g_ref[...] + bg_ref[...]
    lane = lax.broadcasted_iota(I32, (L, LANES), 1)
    gates = jnp.where(lane < N_HEADS, g, _log_sigmoid(g))
    row = lax.broadcasted_iota(I32, (L, L), 0)
    col = lax.broadcasted_iota(I32, (L, L), 1)
    causal = row >= col
    tri = causal.astype(BF16)
    csum = _cumsum_rows(tri, gates)
    gates_t = gates.T
    csum_t = csum.T

    for h in range(N_HEADS):
        sl = slice(h * HEAD_DIM, (h + 1) * HEAD_DIM)
        qh, kh, vh = q_ref[:, sl], k_ref[:, sl], v_ref[:, sl]
        b_col = csum[:, N_HEADS + h:N_HEADS + h + 1]
        li_col = gates[:, h:h + 1]
        b_row = csum_t[N_HEADS + h:N_HEADS + h + 1, :]
        li_row = gates_t[h:h + 1, :]
        m_prev = m_ref[h:h + 1, 0:1]
        c_prev = c_ref[h]
        n_prev = n_ref[h:h + 1, :]

        dm = jnp.where(causal, b_col - b_row + li_row, NEG)
        inter = b_col + m_prev
        m_t = jnp.maximum(inter, jnp.max(dm, axis=-1, keepdims=True))
        w = jnp.exp(dm - m_t) * (_dot_nt(qh, kh) * scale)
        wi = jnp.exp(inter - m_t)
        num = _dot(w.astype(BF16), vh) + wi * _dot(qh, c_prev.astype(BF16))
        den = (jnp.sum(w, axis=-1, keepdims=True)
               + wi * jnp.sum(qh.astype(F32) * n_prev, axis=-1, keepdims=True))
        hval = num / jnp.maximum(jnp.abs(den), jnp.exp(-m_t))

        m_new = m_t[L - 1:L, :]
        b_last = b_col[L - 1:L, :]
        ws = jnp.exp(b_last - b_col + li_col - m_new)
        dec = jnp.exp(b_last + m_prev - m_new)
        kw = kh.astype(F32) * (ws * scale)
        c_ref[h] = dec * c_prev + _dot_tn(kw.astype(BF16), vh)
        n_ref[h:h + 1, :] = dec * n_prev + jnp.sum(kw, axis=0, keepdims=True)
        m_ref[h:h + 1, :] = jnp.broadcast_to(m_new, (1, LANES))

        hn = _rms(hval, na_ref[:, sl]) * _sigmoid(o_ref[:, sl].astype(F32))
        h_ref[:, sl] = hn.astype(h_ref.dtype)


def _mlstm_prompt(pa, pb, b_gate_row, norm_a, bsz, seq):
    L = MLSTM_CHUNK
    nc = seq // L
    n = bsz * seq
    gate_blk = (pb.shape[1] - LANES) // LANES

    def col(j):
        return pl.BlockSpec((L, GROUP_W), lambda b, c: (b * nc + c, j))

    return pl.pallas_call(
        _mlstm_kernel,
        out_shape=(jax.ShapeDtypeStruct((n, GROUP_W), BF16),
                   jax.ShapeDtypeStruct((1, bsz, N_HEADS, HEAD_DIM, HEAD_DIM), F32),
                   jax.ShapeDtypeStruct((1, bsz, N_HEADS, HEAD_DIM), F32),
                   jax.ShapeDtypeStruct((bsz, 8, LANES), F32)),
        grid=(bsz, nc),
        in_specs=[col(0), col(1), col(2), col(3),
                  pl.BlockSpec((L, LANES), lambda b, c: (b * nc + c, gate_blk)),
                  pl.BlockSpec((1, LANES), lambda b, c: (0, 0)),
                  pl.BlockSpec((1, GROUP_W), lambda b, c: (0, 0))],
        out_specs=(pl.BlockSpec((L, GROUP_W), lambda b, c: (b * nc + c, 0)),
                   pl.BlockSpec((None, None, N_HEADS, HEAD_DIM, HEAD_DIM), lambda b, c: (0, b, 0, 0, 0)),
                   pl.BlockSpec((None, None, N_HEADS, HEAD_DIM), lambda b, c: (0, b, 0, 0)),
                   pl.BlockSpec((None, 8, LANES), lambda b, c: (b, 0, 0))),
        compiler_params=_cparams(("parallel", "arbitrary")),
        name="mlstm_prompt",
    )(pa, pa, pa, pa, pb, b_gate_row, norm_a)


def _lower_bound(lb_logits_ref):
    lg = lb_logits_ref[...]
    e = jnp.exp(lg - jnp.max(lg, axis=0, keepdims=True))
    return e[0:1, :] / jnp.sum(e, axis=0, keepdims=True)


def _hgrn_kernel(q_ref, v_ref, g_ref, f_ref, lbl_ref, nb_ref, h_ref, s_ref, st_ref):
    LB = q_ref.shape[0]
    C = HGRN_SUB
    NS = LB // C
    H2 = C // 2
    assert LB == LANES
    scale = HEAD_DIM ** -0.5

    @pl.when(pl.program_id(1) == 0)
    def _():
        st_ref[...] = jnp.zeros_like(st_ref)

    lb = _lower_bound(lbl_ref)
    f = lb + (1.0 - lb) * _sigmoid(f_ref[...])
    qraw = q_ref[...].astype(F32)
    q_all = qraw * _sigmoid(qraw) * scale
    k_all = 1.0 - f
    row = lax.broadcasted_iota(I32, (LB, LB), 0)
    col = lax.broadcasted_iota(I32, (LB, LB), 1)
    b_all = _cumsum_rows((row >= col).astype(BF16), jnp.log(f))

    rl = lax.broadcasted_iota(I32, (C, LANES), 0)
    cl = lax.broadcasted_iota(I32, (C, LANES), 1)
    ones_bf = jnp.ones((HEAD_DIM, LANES), BF16)

    for h in range(N_HEADS):
        sl = slice(h * HEAD_DIM, (h + 1) * HEAD_DIM)
        qh, kh, bh = q_all[:, sl], k_all[:, sl], b_all[:, sl]
        vh = v_ref[:, sl]
        st = st_ref[h]

        parts = []
        for i in range(NS):
            q_i, k_i, b_i = (x[i * C:(i + 1) * C, :] for x in (qh, kh, bh))
            for s in range(C):
                lo = 0 if s < H2 else H2
                parts.append(q_i[lo:, :] * k_i[s:s + 1, :]
                             * jnp.exp(jnp.minimum(b_i[lo:, :] - b_i[s:s + 1, :], 0.0)))
        sums = _dot(jnp.concatenate(parts, axis=0).astype(BF16), ones_bf)

        a_rows = []
        off = 0
        for i in range(NS):
            a = jnp.zeros((C, LANES), F32)
            for s in range(C):
                lo = 0 if s < H2 else H2
                blk = sums[off:off + C - lo, :]
                if lo:
                    blk = jnp.concatenate([jnp.zeros((lo, LANES), F32), blk], axis=0)
                a = jnp.where(cl == i * C + s, blk, a)
                off += C - lo
            a = jnp.where(rl + i * C >= cl, a, 0.0)
            if i > 0:
                b_i = bh[i * C:(i + 1) * C, :]
                r_i = b_i[0:1, :]
                qs = qh[i * C:(i + 1) * C, :] * jnp.exp(b_i - r_i)
                ks = jnp.where(row < i * C, kh * jnp.exp(jnp.minimum(r_i - bh, 0.0)), 0.0)
                a = a + _dot_nt(qs.astype(BF16), ks.astype(BF16))
            a_rows.append(a)
        a_full = jnp.concatenate(a_rows, axis=0)

        o = _dot(a_full.astype(BF16), vh) + _dot_nt((qh * jnp.exp(bh)).astype(BF16), st.astype(BF16))
        b_l = bh[LB - 1:LB, :]
        kd = kh * jnp.exp(b_l - bh)
        st_ref[h] = st * jnp.exp(b_l) + _dot_tn(vh, kd.astype(BF16))

        gv = g_ref[:, sl].astype(F32)
        hn = _rms(o, nb_ref[:, sl]) * (gv * _sigmoid(gv))
        h_ref[:, sl] = hn.astype(h_ref.dtype)

    @pl.when(pl.program_id(1) == pl.num_programs(1) - 1)
    def _():
        for h in range(N_HEADS):
            s_ref[h] = st_ref[h].T


def _hgrn_prompt(pa, pb, lb_logits, norm_b, bsz, seq):
    LB = HGRN_BLOCK
    nc = seq // LB
    n = bsz * seq

    def col(j):
        return pl.BlockSpec((LB, GROUP_W), lambda b, c: (b * nc + c, j))

    return pl.pallas_call(
        _hgrn_kernel,
        out_shape=(jax.ShapeDtypeStruct((n, GROUP_W), BF16),
                   jax.ShapeDtypeStruct((1, bsz, N_HEADS, HEAD_DIM, HEAD_DIM), F32)),
        grid=(bsz, nc),
        in_specs=[col(4), col(5), col(6),
                  pl.BlockSpec((LB, GROUP_W), lambda b, c: (b * nc + c, 0)),
                  pl.BlockSpec(lb_logits.shape, lambda b, c: (0, 0)),
                  pl.BlockSpec((1, GROUP_W), lambda b, c: (0, 0))],
        out_specs=(pl.BlockSpec((LB, GROUP_W), lambda b, c: (b * nc + c, 0)),
                   pl.BlockSpec((None, None, N_HEADS, HEAD_DIM, HEAD_DIM), lambda b, c: (0, b, 0, 0, 0))),
        scratch_shapes=[pltpu.VMEM((N_HEADS, HEAD_DIM, HEAD_DIM), F32)],
        compiler_params=_cparams(("parallel", "arbitrary")),
        name="hgrn_prompt",
    )(pa, pa, pa, pb, lb_logits, norm_b)


def _step_kernel(pa_ref, pb_ref, bg_ref, na_ref, nb_ref, lbl_ref,
                 c0_ref, n0_ref, m0_ref, s0_ref,
                 ha_ref, hb_ref, c1_ref, n1_ref, m1_ref, s1_ref, ta_ref, tb_ref):
    scale = HEAD_DIM ** -0.5
    W = GROUP_W
    H = N_HEADS
    D = HEAD_DIM
    lb = _lower_bound(lbl_ref)
    gates_all = pb_ref[:, W:W + LANES] + bg_ref[...]
    f_all = lb + (1.0 - lb) * _sigmoid(pb_ref[:, 0:W])
    ta_ref[...] = jnp.zeros_like(ta_ref)
    tb_ref[...] = jnp.zeros_like(tb_ref)

    aux_a, aux_b = {}, {}
    for j in range(STEP_BATCH):
        row = slice(j, j + 1)
        for h in range(H):
            q = pa_ref[row, h * D:(h + 1) * D]
            k = pa_ref[row, W + h * D:W + (h + 1) * D]
            li = gates_all[row, h:h + 1]
            lf = _log_sigmoid(gates_all[row, H + h:H + h + 1])
            inter = lf + m0_ref[row, h:h + 1]
            m_t = jnp.maximum(inter, li)
            ws = jnp.exp(li - m_t)
            dec = jnp.exp(inter - m_t)
            kw = k * (ws * scale)
            ta_ref[2 * H * j + h:2 * H * j + h + 1, :] = q
            ta_ref[2 * H * j + H + h:2 * H * j + H + h + 1, :] = kw
            aux_a[j, h] = (q, k, kw, m_t, ws, dec)

            qraw = pa_ref[row, 4 * W + h * D:4 * W + (h + 1) * D]
            qb = qraw * _sigmoid(qraw) * scale
            f = f_all[row, h * D:(h + 1) * D]
            decay = jnp.exp(jnp.log(f))
            kb = 1.0 - f
            tb_ref[3 * H * j + h:3 * H * j + h + 1, :] = qb * decay
            tb_ref[3 * H * j + H + h:3 * H * j + H + h + 1, :] = decay
            tb_ref[3 * H * j + 2 * H + h:3 * H * j + 2 * H + h + 1, :] = kb
            aux_b[j, h] = (qb, kb)

    ta = ta_ref[...].T
    tb = tb_ref[...].T
    for j in range(STEP_BATCH):
        row = slice(j, j + 1)
        for h in range(H):
            sl = slice(h * D, (h + 1) * D)
            q, k, kw, m_t, ws, dec = aux_a[j, h]
            v = pa_ref[row, 2 * W + h * D:2 * W + (h + 1) * D]
            og = pa_ref[row, 3 * W + h * D:3 * W + (h + 1) * D]
            q_col = ta[:, 2 * H * j + h:2 * H * j + h + 1]
            kw_col = ta[:, 2 * H * j + H + h:2 * H * j + H + h + 1]
            c0 = c0_ref[j, h]
            n0 = n0_ref[j, h:h + 1, :]
            w = ws * (jnp.sum(q * k, axis=-1, keepdims=True) * scale)
            num = w * v + dec * jnp.sum(q_col * c0, axis=0, keepdims=True)
            den = w + dec * jnp.sum(q * n0, axis=-1, keepdims=True)
            hval = num / jnp.maximum(jnp.abs(den), jnp.exp(-m_t))
            c1_ref[j, h] = dec * c0 + kw_col * v
            n1_ref[j, h:h + 1, :] = dec * n0 + kw
            m1_ref[row, h:h + 1] = m_t
            ha_ref[row, sl] = _rms(hval, na_ref[:, sl]) * _sigmoid(og)
            qb, kb = aux_b[j, h]
            vb = pa_ref[row, 5 * W + h * D:5 * W + (h + 1) * D]
            gv = pa_ref[row, 6 * W + h * D:6 * W + (h + 1) * D]
            qd_col = tb[:, 3 * H * j + h:3 * H * j + h + 1]
            d_col = tb[:, 3 * H * j + H + h:3 * H * j + H + h + 1]
            k_col = tb[:, 3 * H * j + 2 * H + h:3 * H * j + 2 * H + h + 1]
            s0 = s0_ref[j, h]
            a = jnp.sum(qb * kb, axis=-1, keepdims=True)
            o = a * vb + jnp.sum(qd_col * s0, axis=0, keepdims=True)
            s1_ref[j, h] = d_col * s0 + k_col * vb
            hb_ref[row, sl] = _rms(o, nb_ref[:, sl]) * (gv * _sigmoid(gv))


def _step_mixers(pa, pb, b_gate_row, norm_a, norm_b, lb_logits, c0, n0, m0, s0):
    bs = pa.shape[0]
    sb = STEP_BATCH
    st5 = pl.BlockSpec((None, sb, N_HEADS, HEAD_DIM, HEAD_DIM), lambda i: (0, i, 0, 0, 0))
    st4 = pl.BlockSpec((None, sb, N_HEADS, HEAD_DIM), lambda i: (0, i, 0, 0))
    st3 = pl.BlockSpec((None, sb, N_HEADS), lambda i: (0, i, 0))
    rowblk = lambda w: pl.BlockSpec((sb, w), lambda i: (i, 0))
    const = lambda a: pl.BlockSpec(a.shape, lambda i: (0,) * a.ndim)
    return pl.pallas_call(
        _step_kernel,
        out_shape=(jax.ShapeDtypeStruct((bs, GROUP_W), F32),
                   jax.ShapeDtypeStruct((bs, GROUP_W), F32),
                   jax.ShapeDtypeStruct(c0.shape, F32),
                   jax.ShapeDtypeStruct(n0.shape, F32),
                   jax.ShapeDtypeStruct(m0.shape, F32),
                   jax.ShapeDtypeStruct(s0.shape, F32)),
        grid=(bs // sb,),
        in_specs=[rowblk(pa.shape[1]), rowblk(pb.shape[1]), const(b_gate_row), const(norm_a),
                  const(norm_b), const(lb_logits), st5, st4, st3, st5],
        out_specs=(rowblk(GROUP_W), rowblk(GROUP_W), st5, st4, st3, st5),
        scratch_shapes=[pltpu.VMEM((LANES, LANES), F32), pltpu.VMEM((LANES, LANES), F32)],
        compiler_params=_cparams(("parallel",)),
        name="step_mixers",
    )(pa, pb, b_gate_row, norm_a, norm_b, lb_logits, c0, n0, m0, s0)


def _post_rows(ha_ref, hb_ref, x_ref, g1_ref, sh2_ref, sc2_ref, nf_ref, wo_ref, rw_ref, rb_ref,
               x1_ref, h2_ref, idx_ref, gate_ref):
    tm = x_ref.shape[0]
    mix = (_dot(ha_ref[...].astype(BF16), wo_ref[0:GROUP_W, :])
           + _dot(hb_ref[...].astype(BF16), wo_ref[GROUP_W:2 * GROUP_W, :]))
    x1 = x_ref[...] + g1_ref[...] * mix
    x1_ref[0:tm, :] = x1
    h2 = _rms(x1, nf_ref[...]) * (1.0 + sc2_ref[...]) + sh2_ref[...]
    _store_row_tiles(h2_ref, h2)
    logits = _dot(h2.astype(BF16), rw_ref[...]) + rb_ref[...]
    lane = lax.broadcasted_iota(I32, (tm, LANES), 1)
    lane_f = lane.astype(F32)
    cur = logits
    vals, ids = [], []
    for _ in range(TOP_K):
        mx = jnp.max(cur, axis=-1, keepdims=True)
        am = jnp.min(jnp.where(cur == mx, lane_f, float(LANES)), axis=-1, keepdims=True)
        vals.append(mx)
        ids.append(am)
        cur = jnp.where(lane_f == am, -jnp.inf, cur)
    es = [jnp.exp(v - vals[0]) for v in vals]
    tot = es[0] + es[1] + es[2] + es[3]
    idx_out = jnp.full((tm, LANES), -1.0, F32)
    gate_out = jnp.zeros((tm, LANES), F32)
    for k in range(TOP_K):
        idx_out = jnp.where(lane == k, ids[k], idx_out)
        gate_out = jnp.where(lane == k, es[k] / tot, gate_out)
    idx_ref[0:tm, :] = idx_out.astype(I32)
    gate_ref[0:tm, :] = gate_out


def _post_kernel(ha_ref, hb_ref, x_ref, g1_ref, sh2_ref, sc2_ref,
                 has_ref, hbs_ref, xs_ref, g1s_ref, sh2s_ref, sc2s_ref,
                 nf_ref, wo_ref, rw_ref, rb_ref, x1_ref, h2_ref, idx_ref, gate_ref, *, n_prompt_tiles):
    i = pl.program_id(0)
    shared = (nf_ref, wo_ref, rw_ref, rb_ref, x1_ref, h2_ref, idx_ref, gate_ref)

    @pl.when(i < n_prompt_tiles)
    def _():
        _post_rows(ha_ref, hb_ref, x_ref, g1_ref, sh2_ref, sc2_ref, *shared)

    @pl.when(i == n_prompt_tiles)
    def _():
        _post_rows(has_ref, hbs_ref, xs_ref, g1s_ref, sh2s_ref, sc2s_ref, *shared)


def _post(ha_p, hb_p, x_p, mod_p, ha_s, hb_s, x_s, mod_s, norm_ffn, w_out, rw, rb, tm, rows_per_batch):
    n_p, d = x_p.shape
    n_s = x_s.shape[0]
    assert n_s <= tm
    npt = n_p // tm
    tiles = rows_per_batch // tm
    nbatch = n_p // rows_per_batch
    n_all = n_p + n_s
    prow = lambda w: pl.BlockSpec((tm, w), lambda i: (jnp.minimum(i, npt - 1), 0))
    pmod = lambda j: pl.BlockSpec((None, 1, d), lambda i: (jnp.minimum(i // tiles, nbatch - 1), 0, j))
    smod = lambda j: pl.BlockSpec((n_s, d), lambda i: (0, j))
    const = lambda a: pl.BlockSpec(a.shape, lambda i: (0,) * a.ndim)
    out_blk = lambda w: pl.BlockSpec((tm, w), lambda i: (i, 0))
    return pl.pallas_call(
        functools.partial(_post_kernel, n_prompt_tiles=npt),
        out_shape=(jax.ShapeDtypeStruct((n_all, d), F32),
                   jax.ShapeDtypeStruct((n_all * SUBLANES, LANES), F32),
                   jax.ShapeDtypeStruct((n_all, LANES), I32),
                   jax.ShapeDtypeStruct((n_all, LANES), F32)),
        grid=(npt + 1,),
        in_specs=[prow(GROUP_W), prow(GROUP_W), prow(d), pmod(2), pmod(3), pmod(4),
                  const(ha_s), const(hb_s), const(x_s), smod(2), smod(3), smod(4),
                  const(norm_ffn), const(w_out), const(rw), const(rb)],
        out_specs=(out_blk(d), pl.BlockSpec((tm * SUBLANES, LANES), lambda i: (i, 0)),
                   out_blk(LANES), out_blk(LANES)),
        compiler_params=_cparams(("parallel",)),
        name="post",
    )(ha_p, hb_p, x_p, mod_p, mod_p, mod_p, ha_s, hb_s, x_s, mod_s, mod_s, mod_s, norm_ffn, w_out, rw, rb)


def _route_tile(n):
    return max(r for r in range(LANES, 4 * LANES + 1, LANES) if n % r == 0)


def _rank_kernel(idx_ref, rank_ref, cnt_ref, base_ref):
    R = idx_ref.shape[0]

    @pl.when(pl.program_id(0) == 0)
    def _():
        base_ref[...] = jnp.zeros_like(base_ref)

    idx = idx_ref[...]
    lane = lax.broadcasted_iota(I32, (R, LANES), 1)
    hits = [lane == idx[:, k:k + 1] for k in range(TOP_K)]
    onehot = jnp.zeros((R, LANES), F32)
    for hk in hits:
        onehot = onehot + hk.astype(F32)
    row = lax.broadcasted_iota(I32, (R, R), 0)
    col = lax.broadcasted_iota(I32, (R, R), 1)
    before = _dot((row > col).astype(BF16), onehot.astype(BF16)) + base_ref[...]
    rank = jnp.zeros((R, LANES), F32)
    for k, hk in enumerate(hits):
        rk = jnp.sum(jnp.where(hk, before, 0.0), axis=-1, keepdims=True)
        rank = jnp.where(lane == k, rk, rank)
    rank_ref[...] = rank
    base = base_ref[...] + jnp.sum(onehot, axis=0, keepdims=True)
    base_ref[...] = base
    cnt_ref[...] = base


def _rank(idx):
    n = idx.shape[0]
    R = _route_tile(n)
    return pl.pallas_call(
        _rank_kernel,
        out_shape=(jax.ShapeDtypeStruct((n, LANES), F32), jax.ShapeDtypeStruct((1, LANES), F32)),
        grid=(n // R,),
        in_specs=[pl.BlockSpec((R, LANES), lambda i: (i, 0))],
        out_specs=(pl.BlockSpec((R, LANES), lambda i: (i, 0)), pl.BlockSpec((1, LANES), lambda i: (0, 0))),
        scratch_shapes=[pltpu.VMEM((1, LANES), F32)],
        compiler_params=_cparams(("arbitrary",)),
        name="rank",
    )(idx)


def _dest_kernel(idx_ref, rank_ref, start_ref, dest_ref):
    R = idx_ref.shape[0]
    idx = idx_ref[...]
    lane = lax.broadcasted_iota(I32, (R, LANES), 1)
    dest = rank_ref[...]
    for k in range(TOP_K):
        st = jnp.sum(jnp.where(lane == idx[:, k:k + 1], start_ref[...], 0.0), axis=-1, keepdims=True)
        dest = dest + jnp.where(lane == k, st, 0.0)
    dest_ref[...] = dest.T[0:8, :].astype(I32)


def _dest(idx, rank, start_row):
    n = idx.shape[0]
    R = _route_tile(n)
    return pl.pallas_call(
        _dest_kernel,
        out_shape=jax.ShapeDtypeStruct((8, n), I32),
        grid=(n // R,),
        in_specs=[pl.BlockSpec((R, LANES), lambda i: (i, 0)),
                  pl.BlockSpec((R, LANES), lambda i: (i, 0)),
                  pl.BlockSpec((1, LANES), lambda i: (0, 0))],
        out_specs=pl.BlockSpec((8, R), lambda i: (0, i)),
        compiler_params=_cparams(("parallel",)),
        name="dest",
    )(idx, rank, start_row)


def _dispatch_kernel(zb_ref, dest_hbm, x_ref, xs_hbm, dest_smem, zeros, sem_idx, sem_rows, sem_zero):
    R = dest_smem.shape[2]
    S = SUBLANES
    G = DISPATCH_GROUP
    bm = zeros.shape[0]
    i = pl.program_id(0)
    last = pl.num_programs(0) - 1
    slot = lax.rem(i, 2)

    def dest_load(tile, s):
        return pltpu.make_async_copy(dest_hbm.at[:, pl.ds(pl.multiple_of(tile * R, R), R)],
                                     dest_smem.at[s], sem_idx.at[s])

    @pl.when(i == 0)
    def _():
        zeros[...] = jnp.zeros_like(zeros)

        def zero_copy(j):
            return pltpu.make_async_copy(zeros, xs_hbm.at[pl.ds(pl.multiple_of(zb_ref[j] * bm, bm), bm)], sem_zero)

        def start(j, c):
            @pl.when(zb_ref[j] >= 0)
            def _():
                zero_copy(j).start()
            return c

        def wait(j, c):
            @pl.when(zb_ref[j] >= 0)
            def _():
                zero_copy(j).wait()
            return c

        lax.fori_loop(0, zb_ref.shape[0], start, 0)
        lax.fori_loop(0, zb_ref.shape[0], wait, 0)
        dest_load(0, 0).start()

    dest_load(i, slot).wait()

    @pl.when(i < last)
    def _():
        dest_load(i + 1, 1 - slot).start()

    def row_copies(t, g):
        src = x_ref.at[pl.ds(pl.multiple_of(t * S, S), S)]
        return [pltpu.make_async_copy(src, xs_hbm.at[pl.ds(pl.multiple_of(dest_smem[slot, k, t] * S, S), S)],
                                      sem_rows.at[g % 2])
                for k in range(TOP_K)]

    def issue(g):
        def body(t, c):
            for k, cp in enumerate(row_copies(t, g)):
                cp.start(priority=k % 2)
            return c
        lax.fori_loop(g * G, (g + 1) * G, body, 0)

    def drain(g):
        def body(t, c):
            for cp in row_copies(t, g):
                cp.wait()
            return c
        lax.fori_loop(g * G, (g + 1) * G, body, 0)

    n_groups = R // G
    for g in range(n_groups):
        issue(g)
        if g > 0:
            drain(g - 1)
    drain(n_groups - 1)


def _dispatch(zero_blocks, dest, h2, p_rows):
    S = SUBLANES
    n = h2.shape[0] // S
    R = _route_tile(n)
    assert R % DISPATCH_GROUP == 0
    return pl.pallas_call(
        _dispatch_kernel,
        out_shape=jax.ShapeDtypeStruct((p_rows * S, LANES), F32),
        grid_spec=pltpu.PrefetchScalarGridSpec(
            num_scalar_prefetch=1,
            grid=(n // R,),
            in_specs=[pl.BlockSpec(memory_space=pl.ANY),
                      pl.BlockSpec((R * S, LANES), lambda i, zb: (i, 0))],
            out_specs=pl.BlockSpec(memory_space=pl.ANY),
            scratch_shapes=[pltpu.SMEM((2, 8, R), I32), pltpu.VMEM((EXPERT_BLOCK * S, LANES), F32),
                            pltpu.SemaphoreType.DMA((2,)), pltpu.SemaphoreType.DMA((2,)), pltpu.SemaphoreType.DMA]),
        compiler_params=_cparams(("arbitrary",)),
        name="dispatch",
    )(zero_blocks, dest, h2)


def _expert_kernel(be_ref, nv_ref, x_ref, wgu_ref, bgu_ref, wd_ref, bd_ref, y_ref, wgu_bf, wd_bf):
    i = pl.program_id(0)
    dff = wd_ref.shape[0]

    @pl.when(i < nv_ref[0])
    def _():
        changed = jnp.logical_or(i == 0, be_ref[i] != be_ref[jnp.maximum(i - 1, 0)])

        @pl.when(changed)
        def _():
            rows = 128

            def cast_gu(r, c):
                sl = pl.ds(pl.multiple_of(r * rows, rows), rows)
                wgu_bf[sl, :] = wgu_ref[sl, :].astype(BF16)
                return c

            def cast_d(r, c):
                sl = pl.ds(pl.multiple_of(r * rows, rows), rows)
                wd_bf[sl, :] = wd_ref[sl, :].astype(BF16)
                return c

            lax.fori_loop(0, wgu_ref.shape[0] // rows, cast_gu, 0)
            lax.fori_loop(0, wd_ref.shape[0] // rows, cast_d, 0)

        x = _load_row_tiles(x_ref, x_ref.shape[0] // SUBLANES).astype(BF16)
        g = jnp.minimum(_dot(x, wgu_bf[:, 0:dff]) + bgu_ref[:, 0:dff], SWIGLU_LIMIT)
        u = jnp.clip(_dot(x, wgu_bf[:, dff:2 * dff]) + bgu_ref[:, dff:2 * dff], -SWIGLU_LIMIT, SWIGLU_LIMIT)
        act = (u + 1.0) * (g * _sigmoid(SWIGLU_ALPHA * g))
        _store_row_tiles(y_ref, _dot(act.astype(BF16), wd_bf[...]) + bd_ref[...])

    @pl.when(i >= nv_ref[0])
    def _():
        y_ref[...] = jnp.zeros_like(y_ref)


def _experts(block_e, n_valid, xs, w_gu, b_gu, w_down, b_down):
    S = SUBLANES
    bm = EXPERT_BLOCK
    nb = xs.shape[0] // (bm * S)
    e, d, dff2 = w_gu.shape[1:]
    dff = w_down.shape[2]
    blk = lambda i, be, nv: (jnp.minimum(i, nv[0] - 1), 0)
    return pl.pallas_call(
        _expert_kernel,
        out_shape=jax.ShapeDtypeStruct(xs.shape, F32),
        grid_spec=pltpu.PrefetchScalarGridSpec(
            num_scalar_prefetch=2,
            grid=(nb,),
            in_specs=[pl.BlockSpec((bm * S, LANES), blk),
                      pl.BlockSpec((None, None, d, dff2), lambda i, be, nv: (0, be[i], 0, 0)),
                      pl.BlockSpec((None, 1, dff2), lambda i, be, nv: (be[i], 0, 0)),
                      pl.BlockSpec((None, None, dff, d), lambda i, be, nv: (0, be[i], 0, 0)),
                      pl.BlockSpec((None, 1, d), lambda i, be, nv: (be[i], 0, 0))],
            out_specs=pl.BlockSpec((bm * S, LANES), lambda i, be, nv: (i, 0)),
            scratch_shapes=[pltpu.VMEM((d, dff2), BF16), pltpu.VMEM((dff, d), BF16)]),
        compiler_params=_cparams(("arbitrary",)),
        name="experts",
    )(block_e, n_valid, xs, w_gu, b_gu.reshape(e, 1, dff2), w_down, b_down.reshape(e, 1, d))


def _combine_kernel(dest_hbm, ys_hbm, x1_ref, gate_ref, g2p_ref, g2s_ref, nf_ref,
                    yp_ref, ysm_ref, dest_smem, buf, sem_idx, sem_rows, *, n_prompt_tiles):
    R = x1_ref.shape[0]
    i = pl.program_id(0)
    last = pl.num_programs(0) - 1
    slot = lax.rem(i, 2)

    def dest_load(tile, s):
        return pltpu.make_async_copy(dest_hbm.at[:, pl.ds(pl.multiple_of(tile * R, R), R)],
                                     dest_smem.at[s], sem_idx.at[s])

    S = SUBLANES

    def row_copy(s, src_row, k, t):
        return pltpu.make_async_copy(ys_hbm.at[pl.ds(pl.multiple_of(src_row * S, S), S)],
                                     buf.at[s, k, pl.ds(pl.multiple_of(t * S, S), S)], sem_rows.at[s])

    def gather(s):
        def issue(t, c):
            for k in range(TOP_K):
                row_copy(s, dest_smem[s, k, t], k, t).start(priority=k % 2)
            return c
        lax.fori_loop(0, R, issue, 0)

    @pl.when(i == 0)
    def _():
        dest_load(0, 0).start()
        dest_load(0, 0).wait()

        @pl.when(last > 0)
        def _():
            dest_load(1, 1).start()
        gather(0)

    def drain(t, c):
        for k in range(TOP_K):
            row_copy(slot, dest_smem[slot, k, t], k, t).wait()
        return c

    lax.fori_loop(0, R, drain, 0)

    @pl.when(i < last)
    def _():
        dest_load(i + 1, 1 - slot).wait()
        gather(1 - slot)

        @pl.when(i + 2 <= last)
        def _():
            dest_load(i + 2, slot).start()

    gate = gate_ref[...]
    ff = jnp.zeros(x1_ref.shape, F32)
    for k in range(TOP_K):
        ff = ff + _load_row_tiles(buf, R, lead=(slot, k)) * gate[:, k:k + 1]
    is_prompt = i < n_prompt_tiles

    @pl.when(is_prompt)
    def _():
        yp_ref[...] = _rms(x1_ref[...] + g2p_ref[...] * ff, nf_ref[...])

    @pl.when(jnp.logical_not(is_prompt))
    def _():
        ysm_ref[...] = _rms(x1_ref[...] + g2s_ref[...] * ff, nf_ref[...])


def _combine(dest, ys, x1, gates, mod_p, mod_s, norm_final, n_prompt, rows_per_batch):
    n, d = x1.shape
    R = ROW_TILE
    npt = n_prompt // R
    tiles = rows_per_batch // R
    nbatch = n_prompt // rows_per_batch
    n_s = n - n_prompt
    return pl.pallas_call(
        functools.partial(_combine_kernel, n_prompt_tiles=npt),
        out_shape=(jax.ShapeDtypeStruct((n_prompt, d), F32), jax.ShapeDtypeStruct((n_s, d), F32)),
        grid=(n // R,),
        in_specs=[pl.BlockSpec(memory_space=pl.ANY),
                  pl.BlockSpec(memory_space=pl.ANY),
                  pl.BlockSpec((R, d), lambda i: (i, 0)),
                  pl.BlockSpec((R, LANES), lambda i: (i, 0)),
                  pl.BlockSpec((None, 1, d), lambda i: (jnp.minimum(i // tiles, nbatch - 1), 0, 5)),
                  pl.BlockSpec((R, d), lambda i: (jnp.maximum(i - npt, 0), 5)),
                  pl.BlockSpec((1, d), lambda i: (0, 0))],
        out_specs=(pl.BlockSpec((R, d), lambda i: (jnp.minimum(i, npt - 1), 0)),
                   pl.BlockSpec((R, d), lambda i: (jnp.maximum(i - npt, 0), 0))),
        scratch_shapes=[pltpu.SMEM((2, 8, R), I32), pltpu.VMEM((2, TOP_K, R * SUBLANES, LANES), F32),
                        pltpu.SemaphoreType.DMA((2,)), pltpu.SemaphoreType.DMA((2,))],
        compiler_params=_cparams(("arbitrary",)),
        name="combine",
    )(dest, ys, x1, gates, mod_p, mod_s, norm_final)


def _reorder_w_in(w):
    W = GROUP_W
    g0 = 4 * W
    g1 = g0 + 2 * N_HEADS
    pad = jnp.zeros((w.shape[0], LANES - 2 * N_HEADS), w.dtype)
    cols = [w[:, 0:g0], w[:, g1:g1 + W], w[:, g1 + 2 * W:g1 + 4 * W], w[:, g1 + W:g1 + 2 * W], w[:, g0:g1], pad]
    return jnp.concatenate(cols, axis=1).astype(BF16)


def kernel(x_prompt, x_sample, c_prompt, c_sample, state_mlstm_C, state_mlstm_n, state_mlstm_m,
           state_hgrn_S, w_ada, b_ada, norm_mix, norm_ffn, w_in, b_gate, norm_a, lb_logits, norm_b,
           w_out, router_w, router_b, w_gu, b_gu, w_down, b_down, norm_final):
    bp, seq, d = x_prompt.shape
    bs = x_sample.shape[0]
    assert x_sample.shape[1] == 1 and w_ada.shape[0] == 1 and d == SUBLANES * LANES
    n_p = bp * seq
    n_all = n_p + bs
    n_exp = router_w.shape[2]
    W = GROUP_W
    n_a = 7 * W

    mod = _ada(jnp.concatenate([c_prompt, c_sample], axis=0), w_ada[0], b_ada)
    mod_p = mod[:bp].reshape(bp, 1, 6 * d)
    mod_s = mod[bp:]

    w_r = _reorder_w_in(w_in[0])
    nmix = norm_mix.reshape(1, d)
    tm = min(TOKEN_TILE, seq)
    xp = x_prompt.reshape(n_p, d)
    xs_ = x_sample.reshape(bs, d)
    pa_p, pb_p = _inproj(xp, mod_p, False, nmix, w_r, n_a, tm, seq, BF16)
    pa_s, pb_s = _inproj(xs_, mod_s, True, nmix, w_r, n_a, bs, None, F32)

    bg_row = jnp.pad(b_gate.reshape(1, 2 * N_HEADS), ((0, 0), (0, LANES - 2 * N_HEADS)))
    na = norm_a.reshape(1, W)
    nb_ = norm_b.reshape(1, W)
    ha_p, c_p, nrm_p, m_p = _mlstm_prompt(pa_p, pb_p, bg_row, na, bp, seq)
    hb_p, s_p = _hgrn_prompt(pa_p, pb_p, lb_logits, nb_, bp, seq)
    ha_s, hb_s, c_s, nrm_s, m_s, s_s = _step_mixers(
        pa_s, pb_s, bg_row, na, nb_, lb_logits, state_mlstm_C, state_mlstm_n, state_mlstm_m, state_hgrn_S)

    w_o = w_out[0].astype(BF16)
    rw = jnp.pad(router_w[0], ((0, 0), (0, LANES - n_exp))).astype(BF16)
    rb = jnp.pad(router_b.reshape(1, n_exp), ((0, 0), (0, LANES - n_exp)), constant_values=NEG)
    nffn = norm_ffn.reshape(1, d)
    x1, h2, idx, gates = _post(ha_p, hb_p, xp, mod_p, ha_s, hb_s, xs_, mod_s, nffn, w_o, rw, rb, tm, seq)

    bm = EXPERT_BLOCK
    rank, counts = _rank(idx)
    cnt = counts[0].astype(I32)
    padded = ((cnt + bm - 1) // bm) * bm
    ends = jnp.cumsum(padded)
    start_row = (ends - padded).astype(F32).reshape(1, LANES)
    dest = _dest(idx, rank, start_row)
    n_blocks = (n_all * TOP_K + n_exp * (bm - 1) + bm - 1) // bm
    n_valid = (ends[n_exp - 1] // bm).astype(I32)
    blk_start = jnp.arange(n_blocks, dtype=I32) * bm
    block_e = jnp.sum((ends[None, :n_exp] <= blk_start[:, None]).astype(I32), axis=1)
    block_e = jnp.minimum(block_e, n_exp - 1)
    last_e = block_e[jnp.maximum(n_valid - 1, 0)]
    block_e = jnp.where(jnp.arange(n_blocks) < n_valid, block_e, last_e)

    min_valid = -(-(n_all * TOP_K) // bm)
    group_last = jnp.where(padded[:n_exp] > 0, ends[:n_exp] // bm - 1, -1)
    tail = n_valid + jnp.arange(n_blocks - min_valid, dtype=I32)
    tail = jnp.where(tail < n_blocks, tail, -1)
    zero_blocks = jnp.concatenate([group_last.astype(I32), tail])

    xs_sorted = _dispatch(zero_blocks, dest, h2, n_blocks * bm)
    ys_sorted = _experts(block_e, n_valid.reshape(1), xs_sorted, w_gu, b_gu[0], w_down, b_down[0])
    y_p, y_s = _combine(dest, ys_sorted, x1, gates, mod_p, mod_s, norm_final.reshape(1, d), n_p, seq)

    m_p_out = m_p[:, :N_HEADS, 0][None]
    return (y_p.reshape(bp, seq, d), y_s.reshape(bs, 1, d), c_p, nrm_p, m_p_out, s_p,
            c_s, nrm_s, m_s, s_s)
```

```python
import functools

import jax
import jax.numpy as jnp
from jax import lax
from jax.experimental import pallas as pl
from jax.experimental.pallas import tpu as pltpu

F32 = jnp.float32
BF16 = jnp.bfloat16
I32 = jnp.int32

EPS = 1e-6
NEG = -1e30
SWIGLU_LIMIT = 7.0
SWIGLU_ALPHA = 1.702
TOP_K = 4

LANES = 128
HEAD_DIM = 128
N_HEADS = 4
GROUP_W = N_HEADS * HEAD_DIM
VMEM_LIMIT = 56 * 1024 * 1024

MLSTM_CHUNK = 128
HGRN_BLOCK = 128
HGRN_SUB = 16
TOKEN_TILE = 512
ROW_TILE = 128
EXPERT_BLOCK = 512
STEP_BATCH = 8
DISPATCH_GROUP = 64
DMA_UNROLL = 4
STEP_ROWS = 32


def _cparams(sem, vmem=VMEM_LIMIT):
    return pltpu.CompilerParams(dimension_semantics=sem, vmem_limit_bytes=vmem)


def _dot(a, b):
    return jnp.dot(a, b, preferred_element_type=F32)


def _dot_nt(a, b):
    return lax.dot_general(a, b, (((1,), (1,)), ((), ())), preferred_element_type=F32)


def _dot_tn(a, b):
    return lax.dot_general(a, b, (((0,), (0,)), ((), ())), preferred_element_type=F32)


def _sigmoid(x):
    return 1.0 / (1.0 + jnp.exp(-x))


def _log_sigmoid(x):
    return jnp.minimum(x, 0.0) - jnp.log1p(jnp.exp(-jnp.abs(x)))


def _rms(x, g):
    return x * lax.rsqrt(jnp.mean(x * x, axis=-1, keepdims=True) + EPS) * g


def _cumsum_rows(tri, x):
    hi = x.astype(BF16)
    r1 = x - hi.astype(F32)
    mid = r1.astype(BF16)
    lo = (r1 - mid.astype(F32)).astype(BF16)
    return _dot(tri, hi) + _dot(tri, mid) + _dot(tri, lo)


SUBLANES = 8


def _store_row_tiles(ref, x):
    rows = x.shape[0]
    for c in range(SUBLANES):
        ref[pl.ds(c, rows, stride=SUBLANES), :] = x[:, c * LANES:(c + 1) * LANES]


def _load_row_tiles(ref, rows, lead=()):
    return jnp.concatenate([ref[lead + (pl.ds(c, rows, stride=SUBLANES), slice(None))]
                            for c in range(SUBLANES)], axis=-1)


def _ada_kernel(c_ref, w_ref, b_ref, o_ref):
    c = c_ref[...]
    a = (c * _sigmoid(c)).astype(BF16)
    o_ref[...] = _dot(a, w_ref[...].astype(BF16)) + b_ref[...]


def _ada(c_all, w, b):
    m, d = c_all.shape
    n = w.shape[1]
    tn = 1024
    return pl.pallas_call(
        _ada_kernel,
        out_shape=jax.ShapeDtypeStruct((m, n), F32),
        grid=(n // tn,),
        in_specs=[pl.BlockSpec((m, d), lambda j: (0, 0)),
                  pl.BlockSpec((d, tn), lambda j: (0, j)),
                  pl.BlockSpec((1, tn), lambda j: (0, j))],
        out_specs=pl.BlockSpec((m, tn), lambda j: (0, j)),
        compiler_params=_cparams(("parallel",)),
        name="ada",
    )(c_all, w, b)


def _inproj_kernel(x_ref, sh_ref, sc_ref, nw_ref, w_ref, oa_ref, ob_ref):
    h = _rms(x_ref[...], nw_ref[...]) * (1.0 + sc_ref[...]) + sh_ref[...]
    hb = h.astype(BF16)
    na = oa_ref.shape[1]
    nb = ob_ref.shape[1]
    for j in range(0, na, GROUP_W):
        oa_ref[:, j:j + GROUP_W] = _dot(hb, w_ref[:, j:j + GROUP_W]).astype(oa_ref.dtype)
    ob_ref[:, 0:GROUP_W] = _dot(hb, w_ref[:, na:na + GROUP_W])
    ob_ref[:, GROUP_W:nb] = _dot(hb, w_ref[:, na + GROUP_W:na + nb])


def _inproj(x, mod_rows, mod_is_per_row, norm_w, w_r, n_a, tm, rows_per_batch, out_dtype):
    n, d = x.shape
    n_b = w_r.shape[1] - n_a
    if mod_is_per_row:
        sh_spec = pl.BlockSpec((tm, d), lambda i: (i, 0))
        sc_spec = pl.BlockSpec((tm, d), lambda i: (i, 1))
    else:
        tiles = rows_per_batch // tm
        sh_spec = pl.BlockSpec((None, 1, d), lambda i: (i // tiles, 0, 0))
        sc_spec = pl.BlockSpec((None, 1, d), lambda i: (i // tiles, 0, 1))
    return pl.pallas_call(
        _inproj_kernel,
        out_shape=(jax.ShapeDtypeStruct((n, n_a), out_dtype),
                   jax.ShapeDtypeStruct((n, n_b), F32)),
        grid=(n // tm,),
        in_specs=[pl.BlockSpec((tm, d), lambda i: (i, 0)), sh_spec, sc_spec,
                  pl.BlockSpec((1, d), lambda i: (0, 0)),
                  pl.BlockSpec(w_r.shape, lambda i: (0, 0))],
        out_specs=(pl.BlockSpec((tm, n_a), lambda i: (i, 0)),
                   pl.BlockSpec((tm, n_b), lambda i: (i, 0))),
        compiler_params=_cparams(("parallel",)),
        name="inproj",
    )(x, mod_rows, mod_rows, norm_w, w_r)


def _mlstm_kernel(q_ref, k_ref, v_ref, o_ref, g_ref, bg_ref, na_ref,
                  h_ref, c_ref, n_ref, m_ref):
    L = q_ref.shape[0]
    scale = HEAD_DIM ** -0.5

    @pl.when(pl.program_id(1) == 0)
    def _():
        c_ref[...] = jnp.zeros_like(c_ref)
        n_ref[...] = jnp.zeros_like(n_ref)
        m_ref[...] = jnp.full_like(m_ref, NEG)

    g = g_ref[...] + bg_ref[...]
    lane = lax.broadcasted_iota(I32, (L, LANES), 1)
    gates = jnp.where(lane < N_HEADS, g, _log_sigmoid(g))
    row = lax.broadcasted_iota(I32, (L, L), 0)
    col = lax.broadcasted_iota(I32, (L, L), 1)
    causal = row >= col
    tri = causal.astype(BF16)
    csum = _cumsum_rows(tri, gates)
    gates_t = gates.T
    csum_t = csum.T

    for h in range(N_HEADS):
        sl = slice(h * HEAD_DIM, (h + 1) * HEAD_DIM)
        qh, kh, vh = q_ref[:, sl], k_ref[:, sl], v_ref[:, sl]
        b_col = csum[:, N_HEADS + h:N_HEADS + h + 1]
        li_col = gates[:, h:h + 1]
        b_row = csum_t[N_HEADS + h:N_HEADS + h + 1, :]
        li_row = gates_t[h:h + 1, :]
        m_prev = m_ref[h:h + 1, 0:1]
        c_prev = c_ref[h]
        n_prev = n_ref[h:h + 1, :]

        dm = jnp.where(causal, b_col - b_row + li_row, NEG)
        inter = b_col + m_prev
        m_t = jnp.maximum(inter, jnp.max(dm, axis=-1, keepdims=True))
        w = jnp.exp(dm - m_t) * (_dot_nt(qh, kh) * scale)
        wi = jnp.exp(inter - m_t)
        num = _dot(w.astype(BF16), vh) + wi * _dot(qh, c_prev.astype(BF16))
        den = (jnp.sum(w, axis=-1, keepdims=True)
               + wi * jnp.sum(qh.astype(F32) * n_prev, axis=-1, keepdims=True))
        hval = num / jnp.maximum(jnp.abs(den), jnp.exp(-m_t))

        m_new = m_t[L - 1:L, :]
        b_last = b_col[L - 1:L, :]
        ws = jnp.exp(b_last - b_col + li_col - m_new)
        dec = jnp.exp(b_last + m_prev - m_new)
        kw = kh.astype(F32) * (ws * scale)
        c_ref[h] = dec * c_prev + _dot_tn(kw.astype(BF16), vh)
        n_ref[h:h + 1, :] = dec * n_prev + jnp.sum(kw, axis=0, keepdims=True)
        m_ref[h:h + 1, :] = jnp.broadcast_to(m_new, (1, LANES))

        hn = _rms(hval, na_ref[:, sl]) * _sigmoid(o_ref[:, sl].astype(F32))
        h_ref[:, sl] = hn.astype(h_ref.dtype)


def _mlstm_prompt(pa, pb, b_gate_row, norm_a, bsz, seq):
    L = MLSTM_CHUNK
    nc = seq // L
    n = bsz * seq
    gate_blk = (pb.shape[1] - LANES) // LANES

    def col(j):
        return pl.BlockSpec((L, GROUP_W), lambda b, c: (b * nc + c, j))

    return pl.pallas_call(
        _mlstm_kernel,
        out_shape=(jax.ShapeDtypeStruct((n, GROUP_W), BF16),
                   jax.ShapeDtypeStruct((1, bsz, N_HEADS, HEAD_DIM, HEAD_DIM), F32),
                   jax.ShapeDtypeStruct((1, bsz, N_HEADS, HEAD_DIM), F32),
                   jax.ShapeDtypeStruct((bsz, 8, LANES), F32)),
        grid=(bsz, nc),
        in_specs=[col(0), col(1), col(2), col(3),
                  pl.BlockSpec((L, LANES), lambda b, c: (b * nc + c, gate_blk)),
                  pl.BlockSpec((1, LANES), lambda b, c: (0, 0)),
                  pl.BlockSpec((1, GROUP_W), lambda b, c: (0, 0))],
        out_specs=(pl.BlockSpec((L, GROUP_W), lambda b, c: (b * nc + c, 0)),
                   pl.BlockSpec((None, None, N_HEADS, HEAD_DIM, HEAD_DIM), lambda b, c: (0, b, 0, 0, 0)),
                   pl.BlockSpec((None, None, N_HEADS, HEAD_DIM), lambda b, c: (0, b, 0, 0)),
                   pl.BlockSpec((None, 8, LANES), lambda b, c: (b, 0, 0))),
        compiler_params=_cparams(("parallel", "arbitrary")),
        name="mlstm_prompt",
    )(pa, pa, pa, pa, pb, b_gate_row, norm_a)


def _lower_bound(lb_logits_ref):
    lg = lb_logits_ref[...]
    e = jnp.exp(lg - jnp.max(lg, axis=0, keepdims=True))
    return e[0:1, :] / jnp.sum(e, axis=0, keepdims=True)


def _hgrn_kernel(q_ref, v_ref, g_ref, f_ref, lbl_ref, nb_ref, h_ref, s_ref, st_ref):
    LB = q_ref.shape[0]
    C = HGRN_SUB
    NS = LB // C
    H2 = C // 2
    assert LB == LANES
    scale = HEAD_DIM ** -0.5

    @pl.when(pl.program_id(1) == 0)
    def _():
        st_ref[...] = jnp.zeros_like(st_ref)

    lb = _lower_bound(lbl_ref)
    f = lb + (1.0 - lb) * _sigmoid(f_ref[...])
    qraw = q_ref[...].astype(F32)
    q_all = qraw * _sigmoid(qraw) * scale
    k_all = 1.0 - f
    row = lax.broadcasted_iota(I32, (LB, LB), 0)
    col = lax.broadcasted_iota(I32, (LB, LB), 1)
    b_all = _cumsum_rows((row >= col).astype(BF16), jnp.log(f))

    rl = lax.broadcasted_iota(I32, (C, LANES), 0)
    cl = lax.broadcasted_iota(I32, (C, LANES), 1)
    ones_bf = jnp.ones((HEAD_DIM, LANES), BF16)

    for h in range(N_HEADS):
        sl = slice(h * HEAD_DIM, (h + 1) * HEAD_DIM)
        qh, kh, bh = q_all[:, sl], k_all[:, sl], b_all[:, sl]
        vh = v_ref[:, sl]
        st = st_ref[h]

        parts = []
        for i in range(NS):
            q_i, k_i, b_i = (x[i * C:(i + 1) * C, :] for x in (qh, kh, bh))
            for s in range(C):
                lo = 0 if s < H2 else H2
                parts.append(q_i[lo:, :] * k_i[s:s + 1, :]
                             * jnp.exp(jnp.minimum(b_i[lo:, :] - b_i[s:s + 1, :], 0.0)))
        sums = _dot(jnp.concatenate(parts, axis=0).astype(BF16), ones_bf)

        a_rows = []
        off = 0
        for i in range(NS):
            a = jnp.zeros((C, LANES), F32)
            for s in range(C):
                lo = 0 if s < H2 else H2
                blk = sums[off:off + C - lo, :]
                if lo:
                    blk = jnp.concatenate([jnp.zeros((lo, LANES), F32), blk], axis=0)
                a = jnp.where(cl == i * C + s, blk, a)
                off += C - lo
            a = jnp.where(rl + i * C >= cl, a, 0.0)
            if i > 0:
                b_i = bh[i * C:(i + 1) * C, :]
                r_i = b_i[0:1, :]
                qs = qh[i * C:(i + 1) * C, :] * jnp.exp(b_i - r_i)
                ks = jnp.where(row < i * C, kh * jnp.exp(jnp.minimum(r_i - bh, 0.0)), 0.0)
                a = a + _dot_nt(qs.astype(BF16), ks.astype(BF16))
            a_rows.append(a)
        a_full = jnp.concatenate(a_rows, axis=0)

        o = _dot(a_full.astype(BF16), vh) + _dot_nt((qh * jnp.exp(bh)).astype(BF16), st.astype(BF16))
        b_l = bh[LB - 1:LB, :]
        kd = kh * jnp.exp(b_l - bh)
        st_ref[h] = st * jnp.exp(b_l) + _dot_tn(vh, kd.astype(BF16))

        gv = g_ref[:, sl].astype(F32)
        hn = _rms(o, nb_ref[:, sl]) * (gv * _sigmoid(gv))
        h_ref[:, sl] = hn.astype(h_ref.dtype)

    @pl.when(pl.program_id(1) == pl.num_programs(1) - 1)
    def _():
        for h in range(N_HEADS):
            s_ref[h] = st_ref[h].T


def _hgrn_prompt(pa, pb, lb_logits, norm_b, bsz, seq):
    LB = HGRN_BLOCK
    nc = seq // LB
    n = bsz * seq

    def col(j):
        return pl.BlockSpec((LB, GROUP_W), lambda b, c: (b * nc + c, j))

    return pl.pallas_call(
        _hgrn_kernel,
        out_shape=(jax.ShapeDtypeStruct((n, GROUP_W), BF16),
                   jax.ShapeDtypeStruct((1, bsz, N_HEADS, HEAD_DIM, HEAD_DIM), F32)),
        grid=(bsz, nc),
        in_specs=[col(4), col(5), col(6),
                  pl.BlockSpec((LB, GROUP_W), lambda b, c: (b * nc + c, 0)),
                  pl.BlockSpec(lb_logits.shape, lambda b, c: (0, 0)),
                  pl.BlockSpec((1, GROUP_W), lambda b, c: (0, 0))],
        out_specs=(pl.BlockSpec((LB, GROUP_W), lambda b, c: (b * nc + c, 0)),
                   pl.BlockSpec((None, None, N_HEADS, HEAD_DIM, HEAD_DIM), lambda b, c: (0, b, 0, 0, 0))),
        scratch_shapes=[pltpu.VMEM((N_HEADS, HEAD_DIM, HEAD_DIM), F32)],
        compiler_params=_cparams(("parallel", "arbitrary")),
        name="hgrn_prompt",
    )(pa, pa, pa, pb, lb_logits, norm_b)


def _step_kernel(pa_ref, pb_ref, bg_ref, na_ref, nb_ref, lbl_ref,
                 c0_ref, n0_ref, m0_ref, s0_ref,
                 ha_ref, hb_ref, c1_ref, n1_ref, m1_ref, s1_ref, a_ref, b_ref, q_ref):
    scale = HEAD_DIM ** -0.5
    W = GROUP_W
    H = N_HEADS
    D = HEAD_DIM
    CB = 3 * D
    lb = _lower_bound(lbl_ref)
    gates_all = pb_ref[:, W:W + LANES] + bg_ref[...]
    f_all = lb + (1.0 - lb) * _sigmoid(pb_ref[:, 0:W])
    a_ref[...] = jnp.zeros_like(a_ref)
    b_ref[...] = jnp.zeros_like(b_ref)
    q_ref[...] = jnp.zeros_like(q_ref)
    for h in range(H):
        for part in range(3):
            b_ref[(2 + part) * H + h:(2 + part) * H + h + 1, h * CB + 2 * D:h * CB + 3 * D] = jnp.ones((1, D), F32)

    for j in range(STEP_BATCH):
        row = slice(j, j + 1)
        aux = []
        for h in range(H):
            q = pa_ref[row, h * D:(h + 1) * D]
            k = pa_ref[row, W + h * D:W + (h + 1) * D]
            v = pa_ref[row, 2 * W + h * D:2 * W + (h + 1) * D]
            li = gates_all[row, h:h + 1]
            lf = _log_sigmoid(gates_all[row, H + h:H + h + 1])
            inter = lf + m0_ref[row, h:h + 1]
            m_t = jnp.maximum(inter, li)
            ws = jnp.exp(li - m_t)
            dec = jnp.exp(inter - m_t)
            kw = k * (ws * scale)

            qraw = pa_ref[row, 4 * W + h * D:4 * W + (h + 1) * D]
            qb = qraw * _sigmoid(qraw) * scale
            vb = pa_ref[row, 5 * W + h * D:5 * W + (h + 1) * D]
            f = f_all[row, h * D:(h + 1) * D]
            decay = jnp.exp(jnp.log(f))
            kb = 1.0 - f
            d_hi = decay.astype(BF16).astype(F32)
            d_mid = (decay - d_hi).astype(BF16).astype(F32)

            a_ref[h:h + 1, :] = kw
            a_ref[H + h:H + h + 1, :] = kb
            a_ref[2 * H + h:2 * H + h + 1, :] = d_hi
            a_ref[3 * H + h:3 * H + h + 1, :] = d_mid
            a_ref[4 * H + h:4 * H + h + 1, :] = decay - d_hi - d_mid
            b_ref[h:h + 1, h * CB:h * CB + D] = v
            b_ref[H + h:H + h + 1, h * CB + D:h * CB + 2 * D] = vb
            q_ref[h:h + 1, :] = q
            q_ref[H + h:H + h + 1, :] = qb * decay
            aux.append((q, k, v, kw, m_t, ws, dec, qb, kb, vb))

        upd = _dot_tn(a_ref[...].astype(BF16), b_ref[...].astype(BF16))
        q_rows = q_ref[...].astype(BF16)
        for h in range(H):
            sl = slice(h * D, (h + 1) * D)
            q, k, v, kw, m_t, ws, dec, qb, kb, vb = aux[h]
            og = pa_ref[row, 3 * W + h * D:3 * W + (h + 1) * D]
            c0 = c0_ref[j, h]
            n0 = n0_ref[j, h:h + 1, :]
            w = ws * (jnp.sum(q * k, axis=-1, keepdims=True) * scale)
            num = w * v + dec * _dot(q_rows, c0.astype(BF16))[h:h + 1, :]
            den = w + dec * jnp.sum(q * n0, axis=-1, keepdims=True)
            hval = num / jnp.maximum(jnp.abs(den), jnp.exp(-m_t))
            c1_ref[j, h] = dec * c0 + upd[:, h * CB:h * CB + D]
            n1_ref[j, h:h + 1, :] = dec * n0 + kw
            m1_ref[row, h:h + 1] = m_t
            ha_ref[row, sl] = _rms(hval, na_ref[:, sl]) * _sigmoid(og)
            gv = pa_ref[row, 6 * W + h * D:6 * W + (h + 1) * D]
            s0 = s0_ref[j, h]
            a = jnp.sum(qb * kb, axis=-1, keepdims=True)
            o = a * vb + _dot(q_rows, s0.astype(BF16))[H + h:H + h + 1, :]
            s1_ref[j, h] = upd[:, h * CB + 2 * D:h * CB + 3 * D] * s0 + upd[:, h * CB + D:h * CB + 2 * D]
            hb_ref[row, sl] = _rms(o, nb_ref[:, sl]) * (gv * _sigmoid(gv))


def _step_mixers(pa, pb, b_gate_row, norm_a, norm_b, lb_logits, c0, n0, m0, s0):
    bs = pa.shape[0]
    sb = STEP_BATCH
    st5 = pl.BlockSpec((None, sb, N_HEADS, HEAD_DIM, HEAD_DIM), lambda i: (0, i, 0, 0, 0))
    st4 = pl.BlockSpec((None, sb, N_HEADS, HEAD_DIM), lambda i: (0, i, 0, 0))
    st3 = pl.BlockSpec((None, sb, N_HEADS), lambda i: (0, i, 0))
    rowblk = lambda w: pl.BlockSpec((sb, w), lambda i: (i, 0))
    const = lambda a: pl.BlockSpec(a.shape, lambda i: (0,) * a.ndim)
    return pl.pallas_call(
        _step_kernel,
        out_shape=(jax.ShapeDtypeStruct((bs, GROUP_W), F32),
                   jax.ShapeDtypeStruct((bs, GROUP_W), F32),
                   jax.ShapeDtypeStruct(c0.shape, F32),
                   jax.ShapeDtypeStruct(n0.shape, F32),
                   jax.ShapeDtypeStruct(m0.shape, F32),
                   jax.ShapeDtypeStruct(s0.shape, F32)),
        grid=(bs // sb,),
        in_specs=[rowblk(pa.shape[1]), rowblk(pb.shape[1]), const(b_gate_row), const(norm_a),
                  const(norm_b), const(lb_logits), st5, st4, st3, st5],
        out_specs=(rowblk(GROUP_W), rowblk(GROUP_W), st5, st4, st3, st5),
        scratch_shapes=[pltpu.VMEM((STEP_ROWS, HEAD_DIM), F32),
                        pltpu.VMEM((STEP_ROWS, 3 * N_HEADS * HEAD_DIM), F32),
                        pltpu.VMEM((16, HEAD_DIM), F32)],
        compiler_params=_cparams(("parallel",)),
        name="step_mixers",
    )(pa, pb, b_gate_row, norm_a, norm_b, lb_logits, c0, n0, m0, s0)


def _post_rows(ha_ref, hb_ref, x_ref, g1_ref, sh2_ref, sc2_ref, nf_ref, wo_ref, rw_ref, rb_ref,
               x1_ref, h2_ref, idx_ref, gate_ref):
    tm = x_ref.shape[0]
    mix = (_dot(ha_ref[...].astype(BF16), wo_ref[0:GROUP_W, :])
           + _dot(hb_ref[...].astype(BF16), wo_ref[GROUP_W:2 * GROUP_W, :]))
    x1 = x_ref[...] + g1_ref[...] * mix
    x1_ref[0:tm, :] = x1
    h2 = _rms(x1, nf_ref[...]) * (1.0 + sc2_ref[...]) + sh2_ref[...]
    _store_row_tiles(h2_ref, h2)
    logits = _dot(h2.astype(BF16), rw_ref[...]) + rb_ref[...]
    lane = lax.broadcasted_iota(I32, (tm, LANES), 1)
    lane_f = lane.astype(F32)
    cur = logits
    vals, ids = [], []
    for _ in range(TOP_K):
        mx = jnp.max(cur, axis=-1, keepdims=True)
        am = jnp.min(jnp.where(cur == mx, lane_f, float(LANES)), axis=-1, keepdims=True)
        vals.append(mx)
        ids.append(am)
        cur = jnp.where(lane_f == am, -jnp.inf, cur)
    es = [jnp.exp(v - vals[0]) for v in vals]
    tot = es[0] + es[1] + es[2] + es[3]
    idx_out = jnp.full((tm, LANES), -1.0, F32)
    gate_out = jnp.zeros((tm, LANES), F32)
    for k in range(TOP_K):
        idx_out = jnp.where(lane == k, ids[k], idx_out)
        gate_out = jnp.where(lane == k, es[k] / tot, gate_out)
    idx_ref[0:tm, :] = idx_out.astype(I32)
    gate_ref[0:tm, :] = gate_out


def _post_kernel(ha_ref, hb_ref, x_ref, g1_ref, sh2_ref, sc2_ref,
                 has_ref, hbs_ref, xs_ref, g1s_ref, sh2s_ref, sc2s_ref,
                 nf_ref, wo_ref, rw_ref, rb_ref, x1_ref, h2_ref, idx_ref, gate_ref, *, n_prompt_tiles):
    i = pl.program_id(0)
    shared = (nf_ref, wo_ref, rw_ref, rb_ref, x1_ref, h2_ref, idx_ref, gate_ref)

    @pl.when(i < n_prompt_tiles)
    def _():
        _post_rows(ha_ref, hb_ref, x_ref, g1_ref, sh2_ref, sc2_ref, *shared)

    @pl.when(i == n_prompt_tiles)
    def _():
        _post_rows(has_ref, hbs_ref, xs_ref, g1s_ref, sh2s_ref, sc2s_ref, *shared)


def _post(ha_p, hb_p, x_p, mod_p, ha_s, hb_s, x_s, mod_s, norm_ffn, w_out, rw, rb, tm, rows_per_batch):
    n_p, d = x_p.shape
    n_s = x_s.shape[0]
    assert n_s <= tm
    npt = n_p // tm
    tiles = rows_per_batch // tm
    nbatch = n_p // rows_per_batch
    n_all = n_p + n_s
    prow = lambda w: pl.BlockSpec((tm, w), lambda i: (jnp.minimum(i, npt - 1), 0))
    pmod = lambda j: pl.BlockSpec((None, 1, d), lambda i: (jnp.minimum(i // tiles, nbatch - 1), 0, j))
    smod = lambda j: pl.BlockSpec((n_s, d), lambda i: (0, j))
    const = lambda a: pl.BlockSpec(a.shape, lambda i: (0,) * a.ndim)
    out_blk = lambda w: pl.BlockSpec((tm, w), lambda i: (i, 0))
    return pl.pallas_call(
        functools.partial(_post_kernel, n_prompt_tiles=npt),
        out_shape=(jax.ShapeDtypeStruct((n_all, d), F32),
                   jax.ShapeDtypeStruct((n_all * SUBLANES, LANES), F32),
                   jax.ShapeDtypeStruct((n_all, LANES), I32),
                   jax.ShapeDtypeStruct((n_all, LANES), F32)),
        grid=(npt + 1,),
        in_specs=[prow(GROUP_W), prow(GROUP_W), prow(d), pmod(2), pmod(3), pmod(4),
                  const(ha_s), const(hb_s), const(x_s), smod(2), smod(3), smod(4),
                  const(norm_ffn), const(w_out), const(rw), const(rb)],
        out_specs=(out_blk(d), pl.BlockSpec((tm * SUBLANES, LANES), lambda i: (i, 0)),
                   out_blk(LANES), out_blk(LANES)),
        compiler_params=_cparams(("parallel",)),
        name="post",
    )(ha_p, hb_p, x_p, mod_p, mod_p, mod_p, ha_s, hb_s, x_s, mod_s, mod_s, mod_s, norm_ffn, w_out, rw, rb)


def _route_tile(n):
    return max(r for r in range(LANES, 4 * LANES + 1, LANES) if n % r == 0)


def _rank_kernel(idx_ref, rank_ref, cnt_ref, base_ref):
    R = idx_ref.shape[0]

    @pl.when(pl.program_id(0) == 0)
    def _():
        base_ref[...] = jnp.zeros_like(base_ref)

    idx = idx_ref[...]
    lane = lax.broadcasted_iota(I32, (R, LANES), 1)
    hits = [lane == idx[:, k:k + 1] for k in range(TOP_K)]
    onehot = jnp.zeros((R, LANES), F32)
    for hk in hits:
        onehot = onehot + hk.astype(F32)
    row = lax.broadcasted_iota(I32, (R, R), 0)
    col = lax.broadcasted_iota(I32, (R, R), 1)
    before = _dot((row > col).astype(BF16), onehot.astype(BF16)) + base_ref[...]
    rank = jnp.zeros((R, LANES), F32)
    for k, hk in enumerate(hits):
        rk = jnp.sum(jnp.where(hk, before, 0.0), axis=-1, keepdims=True)
        rank = jnp.where(lane == k, rk, rank)
    rank_ref[...] = rank
    base = base_ref[...] + jnp.sum(onehot, axis=0, keepdims=True)
    base_ref[...] = base
    cnt_ref[...] = base


def _rank(idx):
    n = idx.shape[0]
    R = _route_tile(n)
    return pl.pallas_call(
        _rank_kernel,
        out_shape=(jax.ShapeDtypeStruct((n, LANES), F32), jax.ShapeDtypeStruct((1, LANES), F32)),
        grid=(n // R,),
        in_specs=[pl.BlockSpec((R, LANES), lambda i: (i, 0))],
        out_specs=(pl.BlockSpec((R, LANES), lambda i: (i, 0)), pl.BlockSpec((1, LANES), lambda i: (0, 0))),
        scratch_shapes=[pltpu.VMEM((1, LANES), F32)],
        compiler_params=_cparams(("arbitrary",)),
        name="rank",
    )(idx)


def _dest_kernel(idx_ref, rank_ref, start_ref, dest_ref):
    R = idx_ref.shape[0]
    idx = idx_ref[...]
    lane = lax.broadcasted_iota(I32, (R, LANES), 1)
    dest = rank_ref[...]
    for k in range(TOP_K):
        st = jnp.sum(jnp.where(lane == idx[:, k:k + 1], start_ref[...], 0.0), axis=-1, keepdims=True)
        dest = dest + jnp.where(lane == k, st, 0.0)
    dest_ref[...] = dest.T[0:8, :].astype(I32)


def _dest(idx, rank, start_row):
    n = idx.shape[0]
    R = _route_tile(n)
    return pl.pallas_call(
        _dest_kernel,
        out_shape=jax.ShapeDtypeStruct((8, n), I32),
        grid=(n // R,),
        in_specs=[pl.BlockSpec((R, LANES), lambda i: (i, 0)),
                  pl.BlockSpec((R, LANES), lambda i: (i, 0)),
                  pl.BlockSpec((1, LANES), lambda i: (0, 0))],
        out_specs=pl.BlockSpec((8, R), lambda i: (0, i)),
        compiler_params=_cparams(("parallel",)),
        name="dest",
    )(idx, rank, start_row)


def _dispatch_kernel(zb_ref, dest_hbm, x_ref, xs_hbm, dest_smem, zeros, sem_idx, sem_rows, sem_zero):
    R = dest_smem.shape[2]
    S = SUBLANES
    G = DISPATCH_GROUP
    bm = zeros.shape[0]
    i = pl.program_id(0)
    last = pl.num_programs(0) - 1
    slot = lax.rem(i, 2)

    def dest_load(tile, s):
        return pltpu.make_async_copy(dest_hbm.at[:, pl.ds(pl.multiple_of(tile * R, R), R)],
                                     dest_smem.at[s], sem_idx.at[s])

    @pl.when(i == 0)
    def _():
        zeros[...] = jnp.zeros_like(zeros)

        def zero_copy(j):
            return pltpu.make_async_copy(zeros, xs_hbm.at[pl.ds(pl.multiple_of(zb_ref[j] * bm, bm), bm)], sem_zero)

        def start(j, c):
            @pl.when(zb_ref[j] >= 0)
            def _():
                zero_copy(j).start()
            return c

        def wait(j, c):
            @pl.when(zb_ref[j] >= 0)
            def _():
                zero_copy(j).wait()
            return c

        lax.fori_loop(0, zb_ref.shape[0], start, 0)
        lax.fori_loop(0, zb_ref.shape[0], wait, 0)
        dest_load(0, 0).start()

    dest_load(i, slot).wait()

    @pl.when(i < last)
    def _():
        dest_load(i + 1, 1 - slot).start()

    def row_copies(t, g):
        src = x_ref.at[pl.ds(pl.multiple_of(t * S, S), S)]
        return [pltpu.make_async_copy(src, xs_hbm.at[pl.ds(pl.multiple_of(dest_smem[slot, k, t] * S, S), S)],
                                      sem_rows.at[g % 2])
                for k in range(TOP_K)]

    def issue(g):
        def body(t, c):
            for k, cp in enumerate(row_copies(t, g)):
                cp.start(priority=k % 2)
            return c
        lax.fori_loop(g * G, (g + 1) * G, body, 0, unroll=DMA_UNROLL)

    def drain(g):
        def body(t, c):
            for cp in row_copies(t, g):
                cp.wait()
            return c
        lax.fori_loop(g * G, (g + 1) * G, body, 0, unroll=DMA_UNROLL)

    n_groups = R // G
    for g in range(n_groups):
        issue(g)
        if g > 0:
            drain(g - 1)
    drain(n_groups - 1)


def _dispatch(zero_blocks, dest, h2, p_rows):
    S = SUBLANES
    n = h2.shape[0] // S
    R = _route_tile(n)
    assert R % DISPATCH_GROUP == 0
    return pl.pallas_call(
        _dispatch_kernel,
        out_shape=jax.ShapeDtypeStruct((p_rows * S, LANES), F32),
        grid_spec=pltpu.PrefetchScalarGridSpec(
            num_scalar_prefetch=1,
            grid=(n // R,),
            in_specs=[pl.BlockSpec(memory_space=pl.ANY),
                      pl.BlockSpec((R * S, LANES), lambda i, zb: (i, 0))],
            out_specs=pl.BlockSpec(memory_space=pl.ANY),
            scratch_shapes=[pltpu.SMEM((2, 8, R), I32), pltpu.VMEM((EXPERT_BLOCK * S, LANES), F32),
                            pltpu.SemaphoreType.DMA((2,)), pltpu.SemaphoreType.DMA((2,)), pltpu.SemaphoreType.DMA]),
        compiler_params=_cparams(("arbitrary",)),
        name="dispatch",
    )(zero_blocks, dest, h2)


def _expert_kernel(be_ref, nv_ref, x_ref, wgu_ref, bgu_ref, wd_ref, bd_ref, y_ref, wgu_bf, wd_bf):
    i = pl.program_id(0)
    dff = wd_ref.shape[0]

    @pl.when(i < nv_ref[0])
    def _():
        changed = jnp.logical_or(i == 0, be_ref[i] != be_ref[jnp.maximum(i - 1, 0)])

        @pl.when(changed)
        def _():
            rows = 128

            def cast_gu(r, c):
                sl = pl.ds(pl.multiple_of(r * rows, rows), rows)
                wgu_bf[sl, :] = wgu_ref[sl, :].astype(BF16)
                return c

            def cast_d(r, c):
                sl = pl.ds(pl.multiple_of(r * rows, rows), rows)
                wd_bf[sl, :] = wd_ref[sl, :].astype(BF16)
                return c

            lax.fori_loop(0, wgu_ref.shape[0] // rows, cast_gu, 0)
            lax.fori_loop(0, wd_ref.shape[0] // rows, cast_d, 0)

        x = _load_row_tiles(x_ref, x_ref.shape[0] // SUBLANES).astype(BF16)
        g = jnp.minimum(_dot(x, wgu_bf[:, 0:dff]) + bgu_ref[:, 0:dff], SWIGLU_LIMIT)
        u = jnp.clip(_dot(x, wgu_bf[:, dff:2 * dff]) + bgu_ref[:, dff:2 * dff], -SWIGLU_LIMIT, SWIGLU_LIMIT)
        act = (u + 1.0) * (g * _sigmoid(SWIGLU_ALPHA * g))
        _store_row_tiles(y_ref, _dot(act.astype(BF16), wd_bf[...]) + bd_ref[...])

    @pl.when(i >= nv_ref[0])
    def _():
        y_ref[...] = jnp.zeros_like(y_ref)


def _experts(block_e, n_valid, xs, w_gu, b_gu, w_down, b_down):
    S = SUBLANES
    bm = EXPERT_BLOCK
    nb = xs.shape[0] // (bm * S)
    e, d, dff2 = w_gu.shape[1:]
    dff = w_down.shape[2]
    blk = lambda i, be, nv: (jnp.minimum(i, nv[0] - 1), 0)
    return pl.pallas_call(
        _expert_kernel,
        out_shape=jax.ShapeDtypeStruct(xs.shape, F32),
        grid_spec=pltpu.PrefetchScalarGridSpec(
            num_scalar_prefetch=2,
            grid=(nb,),
            in_specs=[pl.BlockSpec((bm * S, LANES), blk),
                      pl.BlockSpec((None, None, d, dff2), lambda i, be, nv: (0, be[i], 0, 0)),
                      pl.BlockSpec((None, 1, dff2), lambda i, be, nv: (be[i], 0, 0)),
                      pl.BlockSpec((None, None, dff, d), lambda i, be, nv: (0, be[i], 0, 0)),
                      pl.BlockSpec((None, 1, d), lambda i, be, nv: (be[i], 0, 0))],
            out_specs=pl.BlockSpec((bm * S, LANES), lambda i, be, nv: (i, 0)),
            scratch_shapes=[pltpu.VMEM((d, dff2), BF16), pltpu.VMEM((dff, d), BF16)]),
        compiler_params=_cparams(("arbitrary",)),
        name="experts",
    )(block_e, n_valid, xs, w_gu, b_gu.reshape(e, 1, dff2), w_down, b_down.reshape(e, 1, d))


def _combine_kernel(dest_hbm, ys_hbm, x1_ref, gate_ref, g2p_ref, g2s_ref, nf_ref,
                    yp_ref, ysm_ref, dest_smem, buf, sem_idx, sem_rows, *, n_prompt_tiles):
    R = x1_ref.shape[0]
    i = pl.program_id(0)
    last = pl.num_programs(0) - 1
    slot = lax.rem(i, 2)

    def dest_load(tile, s):
        return pltpu.make_async_copy(dest_hbm.at[:, pl.ds(pl.multiple_of(tile * R, R), R)],
                                     dest_smem.at[s], sem_idx.at[s])

    S = SUBLANES

    def row_copy(s, src_row, k, t):
        return pltpu.make_async_copy(ys_hbm.at[pl.ds(pl.multiple_of(src_row * S, S), S)],
                                     buf.at[s, k, pl.ds(pl.multiple_of(t * S, S), S)], sem_rows.at[s])

    def gather(s):
        def issue(t, c):
            for k in range(TOP_K):
                row_copy(s, dest_smem[s, k, t], k, t).start(priority=k % 2)
            return c
        lax.fori_loop(0, R, issue, 0, unroll=DMA_UNROLL)

    @pl.when(i == 0)
    def _():
        dest_load(0, 0).start()
        dest_load(0, 0).wait()

        @pl.when(last > 0)
        def _():
            dest_load(1, 1).start()
        gather(0)

    def drain(t, c):
        for k in range(TOP_K):
            row_copy(slot, dest_smem[slot, k, t], k, t).wait()
        return c

    lax.fori_loop(0, R, drain, 0, unroll=DMA_UNROLL)

    @pl.when(i < last)
    def _():
        dest_load(i + 1, 1 - slot).wait()
        gather(1 - slot)

        @pl.when(i + 2 <= last)
        def _():
            dest_load(i + 2, slot).start()

    gate = gate_ref[...]
    ff = jnp.zeros(x1_ref.shape, F32)
    for k in range(TOP_K):
        ff = ff + _load_row_tiles(buf, R, lead=(slot, k)) * gate[:, k:k + 1]
    is_prompt = i < n_prompt_tiles

    @pl.when(is_prompt)
    def _():
        yp_ref[...] = _rms(x1_ref[...] + g2p_ref[...] * ff, nf_ref[...])

    @pl.when(jnp.logical_not(is_prompt))
    def _():
        ysm_ref[...] = _rms(x1_ref[...] + g2s_ref[...] * ff, nf_ref[...])


def _combine(dest, ys, x1, gates, mod_p, mod_s, norm_final, n_prompt, rows_per_batch):
    n, d = x1.shape
    R = ROW_TILE
    npt = n_prompt // R
    tiles = rows_per_batch // R
    nbatch = n_prompt // rows_per_batch
    n_s = n - n_prompt
    return pl.pallas_call(
        functools.partial(_combine_kernel, n_prompt_tiles=npt),
        out_shape=(jax.ShapeDtypeStruct((n_prompt, d), F32), jax.ShapeDtypeStruct((n_s, d), F32)),
        grid=(n // R,),
        in_specs=[pl.BlockSpec(memory_space=pl.ANY),
                  pl.BlockSpec(memory_space=pl.ANY),
                  pl.BlockSpec((R, d), lambda i: (i, 0)),
                  pl.BlockSpec((R, LANES), lambda i: (i, 0)),
                  pl.BlockSpec((None, 1, d), lambda i: (jnp.minimum(i // tiles, nbatch - 1), 0, 5)),
                  pl.BlockSpec((R, d), lambda i: (jnp.maximum(i - npt, 0), 5)),
                  pl.BlockSpec((1, d), lambda i: (0, 0))],
        out_specs=(pl.BlockSpec((R, d), lambda i: (jnp.minimum(i, npt - 1), 0)),
                   pl.BlockSpec((R, d), lambda i: (jnp.maximum(i - npt, 0), 0))),
        scratch_shapes=[pltpu.SMEM((2, 8, R), I32), pltpu.VMEM((2, TOP_K, R * SUBLANES, LANES), F32),
                        pltpu.SemaphoreType.DMA((2,)), pltpu.SemaphoreType.DMA((2,))],
        compiler_params=_cparams(("arbitrary",)),
        name="combine",
    )(dest, ys, x1, gates, mod_p, mod_s, norm_final)


def _reorder_w_in(w):
    W = GROUP_W
    g0 = 4 * W
    g1 = g0 + 2 * N_HEADS
    pad = jnp.zeros((w.shape[0], LANES - 2 * N_HEADS), w.dtype)
    cols = [w[:, 0:g0], w[:, g1:g1 + W], w[:, g1 + 2 * W:g1 + 4 * W], w[:, g1 + W:g1 + 2 * W], w[:, g0:g1], pad]
    return jnp.concatenate(cols, axis=1).astype(BF16)


def kernel(x_prompt, x_sample, c_prompt, c_sample, state_mlstm_C, state_mlstm_n, state_mlstm_m,
           state_hgrn_S, w_ada, b_ada, norm_mix, norm_ffn, w_in, b_gate, norm_a, lb_logits, norm_b,
           w_out, router_w, router_b, w_gu, b_gu, w_down, b_down, norm_final):
    bp, seq, d = x_prompt.shape
    bs = x_sample.shape[0]
    assert x_sample.shape[1] == 1 and w_ada.shape[0] == 1 and d == SUBLANES * LANES
    n_p = bp * seq
    n_all = n_p + bs
    n_exp = router_w.shape[2]
    W = GROUP_W
    n_a = 7 * W

    mod = _ada(jnp.concatenate([c_prompt, c_sample], axis=0), w_ada[0], b_ada)
    mod_p = mod[:bp].reshape(bp, 1, 6 * d)
    mod_s = mod[bp:]

    w_r = _reorder_w_in(w_in[0])
    nmix = norm_mix.reshape(1, d)
    tm = min(TOKEN_TILE, seq)
    xp = x_prompt.reshape(n_p, d)
    xs_ = x_sample.reshape(bs, d)
    pa_p, pb_p = _inproj(xp, mod_p, False, nmix, w_r, n_a, tm, seq, BF16)
    pa_s, pb_s = _inproj(xs_, mod_s, True, nmix, w_r, n_a, bs, None, F32)

    bg_row = jnp.pad(b_gate.reshape(1, 2 * N_HEADS), ((0, 0), (0, LANES - 2 * N_HEADS)))
    na = norm_a.reshape(1, W)
    nb_ = norm_b.reshape(1, W)
    ha_p, c_p, nrm_p, m_p = _mlstm_prompt(pa_p, pb_p, bg_row, na, bp, seq)
    hb_p, s_p = _hgrn_prompt(pa_p, pb_p, lb_logits, nb_, bp, seq)
    ha_s, hb_s, c_s, nrm_s, m_s, s_s = _step_mixers(
        pa_s, pb_s, bg_row, na, nb_, lb_logits, state_mlstm_C, state_mlstm_n, state_mlstm_m, state_hgrn_S)

    w_o = w_out[0].astype(BF16)
    rw = jnp.pad(router_w[0], ((0, 0), (0, LANES - n_exp))).astype(BF16)
    rb = jnp.pad(router_b.reshape(1, n_exp), ((0, 0), (0, LANES - n_exp)), constant_values=NEG)
    nffn = norm_ffn.reshape(1, d)
    x1, h2, idx, gates = _post(ha_p, hb_p, xp, mod_p, ha_s, hb_s, xs_, mod_s, nffn, w_o, rw, rb, tm, seq)

    bm = EXPERT_BLOCK
    rank, counts = _rank(idx)
    cnt = counts[0].astype(I32)
    padded = ((cnt + bm - 1) // bm) * bm
    ends = jnp.cumsum(padded)
    start_row = (ends - padded).astype(F32).reshape(1, LANES)
    dest = _dest(idx, rank, start_row)
    n_blocks = (n_all * TOP_K + n_exp * (bm - 1) + bm - 1) // bm
    n_valid = (ends[n_exp - 1] // bm).astype(I32)
    blk_start = jnp.arange(n_blocks, dtype=I32) * bm
    block_e = jnp.sum((ends[None, :n_exp] <= blk_start[:, None]).astype(I32), axis=1)
    block_e = jnp.minimum(block_e, n_exp - 1)
    last_e = block_e[jnp.maximum(n_valid - 1, 0)]
    block_e = jnp.where(jnp.arange(n_blocks) < n_valid, block_e, last_e)

    min_valid = -(-(n_all * TOP_K) // bm)
    group_last = jnp.where(padded[:n_exp] > 0, ends[:n_exp] // bm - 1, -1)
    tail = n_valid + jnp.arange(n_blocks - min_valid, dtype=I32)
    tail = jnp.where(tail < n_blocks, tail, -1)
    zero_blocks = jnp.concatenate([group_last.astype(I32), tail])

    xs_sorted = _dispatch(zero_blocks, dest, h2, n_blocks * bm)
    ys_sorted = _experts(block_e, n_valid.reshape(1), xs_sorted, w_gu, b_gu[0], w_down, b_down[0])
    y_p, y_s = _combine(dest, ys_sorted, x1, gates, mod_p, mod_s, norm_final.reshape(1, d), n_p, seq)

    m_p_out = m_p[:, :N_HEADS, 0][None]
    return (y_p.reshape(bp, seq, d), y_s.reshape(bs, 1, d), c_p, nrm_p, m_p_out, s_p,
            c_s, nrm_s, m_s, s_s)
```

```python
import functools

import jax
import jax.numpy as jnp
from jax import lax
from jax.experimental import pallas as pl
from jax.experimental.pallas import tpu as pltpu

F32 = jnp.float32
BF16 = jnp.bfloat16
I32 = jnp.int32

EPS = 1e-6
NEG = -1e30
SWIGLU_LIMIT = 7.0
SWIGLU_ALPHA = 1.702
TOP_K = 4

LANES = 128
HEAD_DIM = 128
N_HEADS = 4
GROUP_W = N_HEADS * HEAD_DIM
VMEM_LIMIT = 56 * 1024 * 1024

MLSTM_CHUNK = 256
HGRN_BLOCK = 128
HGRN_SUB = 16
TOKEN_TILE = 512
ROW_TILE = 128
EXPERT_BLOCK = 512
STEP_BATCH = 8
DISPATCH_GROUP = 64
DMA_UNROLL = 8
STEP_ROWS = 32


def _cparams(sem, vmem=VMEM_LIMIT):
    return pltpu.CompilerParams(dimension_semantics=sem, vmem_limit_bytes=vmem)


def _dot(a, b):
    return jnp.dot(a, b, preferred_element_type=F32)


def _dot_nt(a, b):
    return lax.dot_general(a, b, (((1,), (1,)), ((), ())), preferred_element_type=F32)


def _dot_tn(a, b):
    return lax.dot_general(a, b, (((0,), (0,)), ((), ())), preferred_element_type=F32)


def _sigmoid(x):
    return 1.0 / (1.0 + jnp.exp(-x))


def _log_sigmoid(x):
    return jnp.minimum(x, 0.0) - jnp.log1p(jnp.exp(-jnp.abs(x)))


def _rms(x, g):
    return x * lax.rsqrt(jnp.mean(x * x, axis=-1, keepdims=True) + EPS) * g


def _cumsum_rows(tri, x):
    hi = x.astype(BF16)
    r1 = x - hi.astype(F32)
    mid = r1.astype(BF16)
    lo = (r1 - mid.astype(F32)).astype(BF16)
    return _dot(tri, hi) + _dot(tri, mid) + _dot(tri, lo)


SUBLANES = 8


def _store_row_tiles(ref, x):
    rows = x.shape[0]
    for c in range(SUBLANES):
        ref[pl.ds(c, rows, stride=SUBLANES), :] = x[:, c * LANES:(c + 1) * LANES]


def _load_row_tiles(ref, rows, lead=()):
    return jnp.concatenate([ref[lead + (pl.ds(c, rows, stride=SUBLANES), slice(None))]
                            for c in range(SUBLANES)], axis=-1)


def _ada_kernel(c_ref, w_ref, b_ref, o_ref):
    c = c_ref[...]
    a = (c * _sigmoid(c)).astype(BF16)
    o_ref[...] = _dot(a, w_ref[...].astype(BF16)) + b_ref[...]


def _ada(c_all, w, b):
    m, d = c_all.shape
    n = w.shape[1]
    tn = 1024
    return pl.pallas_call(
        _ada_kernel,
        out_shape=jax.ShapeDtypeStruct((m, n), F32),
        grid=(n // tn,),
        in_specs=[pl.BlockSpec((m, d), lambda j: (0, 0)),
                  pl.BlockSpec((d, tn), lambda j: (0, j)),
                  pl.BlockSpec((1, tn), lambda j: (0, j))],
        out_specs=pl.BlockSpec((m, tn), lambda j: (0, j)),
        compiler_params=_cparams(("parallel",)),
        name="ada",
    )(c_all, w, b)


def _inproj_kernel(x_ref, sh_ref, sc_ref, nw_ref, w_ref, oa_ref, ob_ref):
    h = _rms(x_ref[...], nw_ref[...]) * (1.0 + sc_ref[...]) + sh_ref[...]
    hb = h.astype(BF16)
    na = oa_ref.shape[1]
    nb = ob_ref.shape[1]
    for j in range(0, na, GROUP_W):
        oa_ref[:, j:j + GROUP_W] = _dot(hb, w_ref[:, j:j + GROUP_W]).astype(oa_ref.dtype)
    ob_ref[:, 0:GROUP_W] = _dot(hb, w_ref[:, na:na + GROUP_W])
    ob_ref[:, GROUP_W:nb] = _dot(hb, w_ref[:, na + GROUP_W:na + nb])


def _inproj(x, mod_rows, mod_is_per_row, norm_w, w_r, n_a, tm, rows_per_batch, out_dtype):
    n, d = x.shape
    n_b = w_r.shape[1] - n_a
    if mod_is_per_row:
        sh_spec = pl.BlockSpec((tm, d), lambda i: (i, 0))
        sc_spec = pl.BlockSpec((tm, d), lambda i: (i, 1))
    else:
        tiles = rows_per_batch // tm
        sh_spec = pl.BlockSpec((None, 1, d), lambda i: (i // tiles, 0, 0))
        sc_spec = pl.BlockSpec((None, 1, d), lambda i: (i // tiles, 0, 1))
    return pl.pallas_call(
        _inproj_kernel,
        out_shape=(jax.ShapeDtypeStruct((n, n_a), out_dtype),
                   jax.ShapeDtypeStruct((n, n_b), F32)),
        grid=(n // tm,),
        in_specs=[pl.BlockSpec((tm, d), lambda i: (i, 0)), sh_spec, sc_spec,
                  pl.BlockSpec((1, d), lambda i: (0, 0)),
                  pl.BlockSpec(w_r.shape, lambda i: (0, 0))],
        out_specs=(pl.BlockSpec((tm, n_a), lambda i: (i, 0)),
                   pl.BlockSpec((tm, n_b), lambda i: (i, 0))),
        compiler_params=_cparams(("parallel",)),
        name="inproj",
    )(x, mod_rows, mod_rows, norm_w, w_r)


def _mlstm_kernel(q_ref, k_ref, v_ref, o_ref, g_ref, bg_ref, na_ref,
                  h_ref, c_ref, n_ref, m_ref):
    L = q_ref.shape[0]
    scale = HEAD_DIM ** -0.5

    @pl.when(pl.program_id(1) == 0)
    def _():
        c_ref[...] = jnp.zeros_like(c_ref)
        n_ref[...] = jnp.zeros_like(n_ref)
        m_ref[...] = jnp.full_like(m_ref, NEG)

    g = g_ref[...] + bg_ref[...]
    lane = lax.broadcasted_iota(I32, (L, LANES), 1)
    gates = jnp.where(lane < N_HEADS, g, _log_sigmoid(g))
    row = lax.broadcasted_iota(I32, (L, L), 0)
    col = lax.broadcasted_iota(I32, (L, L), 1)
    causal = row >= col
    tri = causal.astype(BF16)
    csum = _cumsum_rows(tri, gates)
    gates_t = gates.T
    csum_t = csum.T

    for h in range(N_HEADS):
        sl = slice(h * HEAD_DIM, (h + 1) * HEAD_DIM)
        qh, kh, vh = q_ref[:, sl], k_ref[:, sl], v_ref[:, sl]
        b_col = csum[:, N_HEADS + h:N_HEADS + h + 1]
        li_col = gates[:, h:h + 1]
        b_row = csum_t[N_HEADS + h:N_HEADS + h + 1, :]
        li_row = gates_t[h:h + 1, :]
        m_prev = m_ref[h:h + 1, 0:1]
        c_prev = c_ref[h]
        n_prev = n_ref[h:h + 1, :]

        dm = jnp.where(causal, b_col - b_row + li_row, NEG)
        inter = b_col + m_prev
        m_t = jnp.maximum(inter, jnp.max(dm, axis=-1, keepdims=True))
        w = jnp.exp(dm - m_t) * (_dot_nt(qh, kh) * scale)
        wi = jnp.exp(inter - m_t)
        num = _dot(w.astype(BF16), vh) + wi * _dot(qh, c_prev.astype(BF16))
        den = (jnp.sum(w, axis=-1, keepdims=True)
               + wi * jnp.sum(qh.astype(F32) * n_prev, axis=-1, keepdims=True))
        hval = num / jnp.maximum(jnp.abs(den), jnp.exp(-m_t))

        m_new = m_t[L - 1:L, :]
        b_last = b_col[L - 1:L, :]
        ws = jnp.exp(b_last - b_col + li_col - m_new)
        dec = jnp.exp(b_last + m_prev - m_new)
        kw = kh.astype(F32) * (ws * scale)
        c_ref[h] = dec * c_prev + _dot_tn(kw.astype(BF16), vh)
        n_ref[h:h + 1, :] = dec * n_prev + jnp.sum(kw, axis=0, keepdims=True)
        m_ref[h:h + 1, :] = jnp.broadcast_to(m_new, (1, LANES))

        hn = _rms(hval, na_ref[:, sl]) * _sigmoid(o_ref[:, sl].astype(F32))
        h_ref[:, sl] = hn.astype(h_ref.dtype)


def _mlstm_prompt(pa, pb, b_gate_row, norm_a, bsz, seq):
    L = MLSTM_CHUNK
    nc = seq // L
    n = bsz * seq
    gate_blk = (pb.shape[1] - LANES) // LANES

    def col(j):
        return pl.BlockSpec((L, GROUP_W), lambda b, c: (b * nc + c, j))

    return pl.pallas_call(
        _mlstm_kernel,
        out_shape=(jax.ShapeDtypeStruct((n, GROUP_W), BF16),
                   jax.ShapeDtypeStruct((1, bsz, N_HEADS, HEAD_DIM, HEAD_DIM), F32),
                   jax.ShapeDtypeStruct((1, bsz, N_HEADS, HEAD_DIM), F32),
                   jax.ShapeDtypeStruct((bsz, 8, LANES), F32)),
        grid=(bsz, nc),
        in_specs=[col(0), col(1), col(2), col(3),
                  pl.BlockSpec((L, LANES), lambda b, c: (b * nc + c, gate_blk)),
                  pl.BlockSpec((1, LANES), lambda b, c: (0, 0)),
                  pl.BlockSpec((1, GROUP_W), lambda b, c: (0, 0))],
        out_specs=(pl.BlockSpec((L, GROUP_W), lambda b, c: (b * nc + c, 0)),
                   pl.BlockSpec((None, None, N_HEADS, HEAD_DIM, HEAD_DIM), lambda b, c: (0, b, 0, 0, 0)),
                   pl.BlockSpec((None, None, N_HEADS, HEAD_DIM), lambda b, c: (0, b, 0, 0)),
                   pl.BlockSpec((None, 8, LANES), lambda b, c: (b, 0, 0))),
        compiler_params=_cparams(("parallel", "arbitrary")),
        name="mlstm_prompt",
    )(pa, pa, pa, pa, pb, b_gate_row, norm_a)


def _lower_bound(lb_logits_ref):
    lg = lb_logits_ref[...]
    e = jnp.exp(lg - jnp.max(lg, axis=0, keepdims=True))
    return e[0:1, :] / jnp.sum(e, axis=0, keepdims=True)


def _hgrn_kernel(q_ref, v_ref, g_ref, f_ref, lbl_ref, nb_ref, h_ref, s_ref, st_ref):
    LB = q_ref.shape[0]
    C = HGRN_SUB
    NS = LB // C
    H2 = C // 2
    assert LB == LANES
    scale = HEAD_DIM ** -0.5

    @pl.when(pl.program_id(1) == 0)
    def _():
        st_ref[...] = jnp.zeros_like(st_ref)

    lb = _lower_bound(lbl_ref)
    f = lb + (1.0 - lb) * _sigmoid(f_ref[...])
    qraw = q_ref[...].astype(F32)
    q_all = qraw * _sigmoid(qraw) * scale
    k_all = 1.0 - f
    row = lax.broadcasted_iota(I32, (LB, LB), 0)
    col = lax.broadcasted_iota(I32, (LB, LB), 1)
    b_all = _cumsum_rows((row >= col).astype(BF16), jnp.log(f))

    rl = lax.broadcasted_iota(I32, (C, LANES), 0)
    cl = lax.broadcasted_iota(I32, (C, LANES), 1)
    ones_bf = jnp.ones((HEAD_DIM, LANES), BF16)

    for h in range(N_HEADS):
        sl = slice(h * HEAD_DIM, (h + 1) * HEAD_DIM)
        qh, kh, bh = q_all[:, sl], k_all[:, sl], b_all[:, sl]
        vh = v_ref[:, sl]
        st = st_ref[h]

        parts = []
        for i in range(NS):
            q_i, k_i, b_i = (x[i * C:(i + 1) * C, :] for x in (qh, kh, bh))
            for s in range(C):
                lo = 0 if s < H2 else H2
                parts.append(q_i[lo:, :] * k_i[s:s + 1, :]
                             * jnp.exp(jnp.minimum(b_i[lo:, :] - b_i[s:s + 1, :], 0.0)))
        sums = _dot(jnp.concatenate(parts, axis=0).astype(BF16), ones_bf)

        a_rows = []
        off = 0
        for i in range(NS):
            a = jnp.zeros((C, LANES), F32)
            for s in range(C):
                lo = 0 if s < H2 else H2
                blk = sums[off:off + C - lo, :]
                if lo:
                    blk = jnp.concatenate([jnp.zeros((lo, LANES), F32), blk], axis=0)
                a = jnp.where(cl == i * C + s, blk, a)
                off += C - lo
            a = jnp.where(rl + i * C >= cl, a, 0.0)
            if i > 0:
                b_i = bh[i * C:(i + 1) * C, :]
                r_i = b_i[0:1, :]
                qs = qh[i * C:(i + 1) * C, :] * jnp.exp(b_i - r_i)
                ks = kh[0:i * C, :] * jnp.exp(jnp.minimum(r_i - bh[0:i * C, :], 0.0))
                ks = jnp.concatenate([ks, jnp.zeros((LB - i * C, HEAD_DIM), F32)], axis=0)
                a = a + _dot_nt(qs.astype(BF16), ks.astype(BF16))
            a_rows.append(a)
        a_full = jnp.concatenate(a_rows, axis=0)

        o = _dot(a_full.astype(BF16), vh) + _dot_nt((qh * jnp.exp(bh)).astype(BF16), st.astype(BF16))
        b_l = bh[LB - 1:LB, :]
        kd = kh * jnp.exp(b_l - bh)
        st_ref[h] = st * jnp.exp(b_l) + _dot_tn(vh, kd.astype(BF16))

        gv = g_ref[:, sl].astype(F32)
        hn = _rms(o, nb_ref[:, sl]) * (gv * _sigmoid(gv))
        h_ref[:, sl] = hn.astype(h_ref.dtype)

    @pl.when(pl.program_id(1) == pl.num_programs(1) - 1)
    def _():
        for h in range(N_HEADS):
            s_ref[h] = st_ref[h].T


def _hgrn_prompt(pa, pb, lb_logits, norm_b, bsz, seq):
    LB = HGRN_BLOCK
    nc = seq // LB
    n = bsz * seq

    def col(j):
        return pl.BlockSpec((LB, GROUP_W), lambda b, c: (b * nc + c, j))

    return pl.pallas_call(
        _hgrn_kernel,
        out_shape=(jax.ShapeDtypeStruct((n, GROUP_W), BF16),
                   jax.ShapeDtypeStruct((1, bsz, N_HEADS, HEAD_DIM, HEAD_DIM), F32)),
        grid=(bsz, nc),
        in_specs=[col(4), col(5), col(6),
                  pl.BlockSpec((LB, GROUP_W), lambda b, c: (b * nc + c, 0)),
                  pl.BlockSpec(lb_logits.shape, lambda b, c: (0, 0)),
                  pl.BlockSpec((1, GROUP_W), lambda b, c: (0, 0))],
        out_specs=(pl.BlockSpec((LB, GROUP_W), lambda b, c: (b * nc + c, 0)),
                   pl.BlockSpec((None, None, N_HEADS, HEAD_DIM, HEAD_DIM), lambda b, c: (0, b, 0, 0, 0))),
        scratch_shapes=[pltpu.VMEM((N_HEADS, HEAD_DIM, HEAD_DIM), F32)],
        compiler_params=_cparams(("parallel", "arbitrary")),
        name="hgrn_prompt",
    )(pa, pa, pa, pb, lb_logits, norm_b)


def _step_kernel(pa_ref, pb_ref, bg_ref, na_ref, nb_ref, lbl_ref,
                 c0_ref, n0_ref, m0_ref, s0_ref,
                 ha_ref, hb_ref, c1_ref, n1_ref, m1_ref, s1_ref, a_ref, b_ref, q_ref):
    scale = HEAD_DIM ** -0.5
    W = GROUP_W
    H = N_HEADS
    D = HEAD_DIM
    CB = 3 * D
    lb = _lower_bound(lbl_ref)
    gates_all = pb_ref[:, W:W + LANES] + bg_ref[...]
    f_all = lb + (1.0 - lb) * _sigmoid(pb_ref[:, 0:W])
    a_ref[...] = jnp.zeros_like(a_ref)
    b_ref[...] = jnp.zeros_like(b_ref)
    q_ref[...] = jnp.zeros_like(q_ref)
    for j in range(STEP_BATCH):
        for h in range(H):
            for part in range(3):
                b_ref[j, (2 + part) * H + h:(2 + part) * H + h + 1, h * CB + 2 * D:h * CB + 3 * D] = (
                    jnp.ones((1, D), F32))

    for j in range(STEP_BATCH):
        row = slice(j, j + 1)
        aux = []
        for h in range(H):
            q = pa_ref[row, h * D:(h + 1) * D]
            k = pa_ref[row, W + h * D:W + (h + 1) * D]
            v = pa_ref[row, 2 * W + h * D:2 * W + (h + 1) * D]
            li = gates_all[row, h:h + 1]
            lf = _log_sigmoid(gates_all[row, H + h:H + h + 1])
            inter = lf + m0_ref[row, h:h + 1]
            m_t = jnp.maximum(inter, li)
            ws = jnp.exp(li - m_t)
            dec = jnp.exp(inter - m_t)
            kw = k * (ws * scale)

            qraw = pa_ref[row, 4 * W + h * D:4 * W + (h + 1) * D]
            qb = qraw * _sigmoid(qraw) * scale
            vb = pa_ref[row, 5 * W + h * D:5 * W + (h + 1) * D]
            f = f_all[row, h * D:(h + 1) * D]
            decay = jnp.exp(jnp.log(f))
            kb = 1.0 - f
            d_hi = decay.astype(BF16).astype(F32)
            d_mid = (decay - d_hi).astype(BF16).astype(F32)

            a_ref[j, h:h + 1, :] = kw
            a_ref[j, H + h:H + h + 1, :] = kb
            a_ref[j, 2 * H + h:2 * H + h + 1, :] = d_hi
            a_ref[j, 3 * H + h:3 * H + h + 1, :] = d_mid
            a_ref[j, 4 * H + h:4 * H + h + 1, :] = decay - d_hi - d_mid
            b_ref[j, h:h + 1, h * CB:h * CB + D] = v
            b_ref[j, H + h:H + h + 1, h * CB + D:h * CB + 2 * D] = vb
            q_ref[j, h:h + 1, :] = q
            q_ref[j, H + h:H + h + 1, :] = qb * decay
            aux.append((q, k, v, kw, m_t, ws, dec, qb, kb, vb))

        upd = _dot_tn(a_ref[j].astype(BF16), b_ref[j].astype(BF16))
        q_rows = q_ref[j].astype(BF16)
        for h in range(H):
            sl = slice(h * D, (h + 1) * D)
            q, k, v, kw, m_t, ws, dec, qb, kb, vb = aux[h]
            og = pa_ref[row, 3 * W + h * D:3 * W + (h + 1) * D]
            c0 = c0_ref[j, h]
            n0 = n0_ref[j, h:h + 1, :]
            w = ws * (jnp.sum(q * k, axis=-1, keepdims=True) * scale)
            num = w * v + dec * _dot(q_rows, c0.astype(BF16))[h:h + 1, :]
            den = w + dec * jnp.sum(q * n0, axis=-1, keepdims=True)
            hval = num / jnp.maximum(jnp.abs(den), jnp.exp(-m_t))
            c1_ref[j, h] = dec * c0 + upd[:, h * CB:h * CB + D]
            n1_ref[j, h:h + 1, :] = dec * n0 + kw
            m1_ref[row, h:h + 1] = m_t
            ha_ref[row, sl] = _rms(hval, na_ref[:, sl]) * _sigmoid(og)
            gv = pa_ref[row, 6 * W + h * D:6 * W + (h + 1) * D]
            s0 = s0_ref[j, h]
            a = jnp.sum(qb * kb, axis=-1, keepdims=True)
            o = a * vb + _dot(q_rows, s0.astype(BF16))[H + h:H + h + 1, :]
            s1_ref[j, h] = upd[:, h * CB + 2 * D:h * CB + 3 * D] * s0 + upd[:, h * CB + D:h * CB + 2 * D]
            hb_ref[row, sl] = _rms(o, nb_ref[:, sl]) * (gv * _sigmoid(gv))


def _step_mixers(pa, pb, b_gate_row, norm_a, norm_b, lb_logits, c0, n0, m0, s0):
    bs = pa.shape[0]
    sb = STEP_BATCH
    st5 = pl.BlockSpec((None, sb, N_HEADS, HEAD_DIM, HEAD_DIM), lambda i: (0, i, 0, 0, 0))
    st4 = pl.BlockSpec((None, sb, N_HEADS, HEAD_DIM), lambda i: (0, i, 0, 0))
    st3 = pl.BlockSpec((None, sb, N_HEADS), lambda i: (0, i, 0))
    rowblk = lambda w: pl.BlockSpec((sb, w), lambda i: (i, 0))
    const = lambda a: pl.BlockSpec(a.shape, lambda i: (0,) * a.ndim)
    return pl.pallas_call(
        _step_kernel,
        out_shape=(jax.ShapeDtypeStruct((bs, GROUP_W), F32),
                   jax.ShapeDtypeStruct((bs, GROUP_W), F32),
                   jax.ShapeDtypeStruct(c0.shape, F32),
                   jax.ShapeDtypeStruct(n0.shape, F32),
                   jax.ShapeDtypeStruct(m0.shape, F32),
                   jax.ShapeDtypeStruct(s0.shape, F32)),
        grid=(bs // sb,),
        in_specs=[rowblk(pa.shape[1]), rowblk(pb.shape[1]), const(b_gate_row), const(norm_a),
                  const(norm_b), const(lb_logits), st5, st4, st3, st5],
        out_specs=(rowblk(GROUP_W), rowblk(GROUP_W), st5, st4, st3, st5),
        scratch_shapes=[pltpu.VMEM((sb, STEP_ROWS, HEAD_DIM), F32),
                        pltpu.VMEM((sb, STEP_ROWS, 3 * N_HEADS * HEAD_DIM), F32),
                        pltpu.VMEM((sb, 16, HEAD_DIM), F32)],
        compiler_params=_cparams(("parallel",)),
        name="step_mixers",
    )(pa, pb, b_gate_row, norm_a, norm_b, lb_logits, c0, n0, m0, s0)


def _post_rows(ha_ref, hb_ref, x_ref, g1_ref, sh2_ref, sc2_ref, nf_ref, wo_ref, rw_ref, rb_ref,
               x1_ref, h2_ref, idx_ref, gate_ref, rank_ref, cnt_ref, base_ref):
    tm = x_ref.shape[0]
    mix = (_dot(ha_ref[...].astype(BF16), wo_ref[0:GROUP_W, :])
           + _dot(hb_ref[...].astype(BF16), wo_ref[GROUP_W:2 * GROUP_W, :]))
    x1 = x_ref[...] + g1_ref[...] * mix
    x1_ref[0:tm, :] = x1
    h2 = _rms(x1, nf_ref[...]) * (1.0 + sc2_ref[...]) + sh2_ref[...]
    _store_row_tiles(h2_ref, h2)
    logits = _dot(h2.astype(BF16), rw_ref[...]) + rb_ref[...]
    lane = lax.broadcasted_iota(I32, (tm, LANES), 1)
    lane_f = lane.astype(F32)
    cur = logits
    vals, ids = [], []
    for _ in range(TOP_K):
        mx = jnp.max(cur, axis=-1, keepdims=True)
        am = jnp.min(jnp.where(cur == mx, lane_f, float(LANES)), axis=-1, keepdims=True)
        vals.append(mx)
        ids.append(am)
        cur = jnp.where(lane_f == am, -jnp.inf, cur)
    es = [jnp.exp(v - vals[0]) for v in vals]
    tot = es[0] + es[1] + es[2] + es[3]
    idx_out = jnp.full((tm, LANES), -1.0, F32)
    gate_out = jnp.zeros((tm, LANES), F32)
    for k in range(TOP_K):
        idx_out = jnp.where(lane == k, ids[k], idx_out)
        gate_out = jnp.where(lane == k, es[k] / tot, gate_out)
    idx_ref[0:tm, :] = idx_out.astype(I32)
    gate_ref[0:tm, :] = gate_out

    hits = [lane_f == ids[k] for k in range(TOP_K)]
    onehot = jnp.zeros((tm, LANES), F32)
    for hk in hits:
        onehot = onehot + hk.astype(F32)
    row = lax.broadcasted_iota(I32, (tm, tm), 0)
    col = lax.broadcasted_iota(I32, (tm, tm), 1)
    before = _dot((row > col).astype(BF16), onehot.astype(BF16)) + base_ref[...]
    rank = jnp.zeros((tm, LANES), F32)
    for k, hk in enumerate(hits):
        rank = jnp.where(lane == k, jnp.sum(jnp.where(hk, before, 0.0), axis=-1, keepdims=True), rank)
    rank_ref[0:tm, :] = rank
    base = base_ref[...] + jnp.sum(onehot, axis=0, keepdims=True)
    base_ref[...] = base
    cnt_ref[...] = base


def _post_kernel(ha_ref, hb_ref, x_ref, g1_ref, sh2_ref, sc2_ref,
                 has_ref, hbs_ref, xs_ref, g1s_ref, sh2s_ref, sc2s_ref,
                 nf_ref, wo_ref, rw_ref, rb_ref, x1_ref, h2_ref, idx_ref, gate_ref, rank_ref, cnt_ref,
                 base_ref, *, n_prompt_tiles):
    i = pl.program_id(0)
    shared = (nf_ref, wo_ref, rw_ref, rb_ref, x1_ref, h2_ref, idx_ref, gate_ref, rank_ref, cnt_ref, base_ref)

    @pl.when(i == 0)
    def _():
        base_ref[...] = jnp.zeros_like(base_ref)

    @pl.when(i < n_prompt_tiles)
    def _():
        _post_rows(ha_ref, hb_ref, x_ref, g1_ref, sh2_ref, sc2_ref, *shared)

    @pl.when(i == n_prompt_tiles)
    def _():
        _post_rows(has_ref, hbs_ref, xs_ref, g1s_ref, sh2s_ref, sc2s_ref, *shared)


def _post(ha_p, hb_p, x_p, mod_p, ha_s, hb_s, x_s, mod_s, norm_ffn, w_out, rw, rb, tm, rows_per_batch):
    n_p, d = x_p.shape
    n_s = x_s.shape[0]
    assert n_s <= tm
    npt = n_p // tm
    tiles = rows_per_batch // tm
    nbatch = n_p // rows_per_batch
    n_all = n_p + n_s
    prow = lambda w: pl.BlockSpec((tm, w), lambda i: (jnp.minimum(i, npt - 1), 0))
    pmod = lambda j: pl.BlockSpec((None, 1, d), lambda i: (jnp.minimum(i // tiles, nbatch - 1), 0, j))
    smod = lambda j: pl.BlockSpec((n_s, d), lambda i: (0, j))
    const = lambda a: pl.BlockSpec(a.shape, lambda i: (0,) * a.ndim)
    out_blk = lambda w: pl.BlockSpec((tm, w), lambda i: (i, 0))
    return pl.pallas_call(
        functools.partial(_post_kernel, n_prompt_tiles=npt),
        out_shape=(jax.ShapeDtypeStruct((n_all, d), F32),
                   jax.ShapeDtypeStruct((n_all * SUBLANES, LANES), F32),
                   jax.ShapeDtypeStruct((n_all, LANES), I32),
                   jax.ShapeDtypeStruct((n_all, LANES), F32),
                   jax.ShapeDtypeStruct((n_all, LANES), F32),
                   jax.ShapeDtypeStruct((1, LANES), F32)),
        grid=(npt + 1,),
        in_specs=[prow(GROUP_W), prow(GROUP_W), prow(d), pmod(2), pmod(3), pmod(4),
                  const(ha_s), const(hb_s), const(x_s), smod(2), smod(3), smod(4),
                  const(norm_ffn), const(w_out), const(rw), const(rb)],
        out_specs=(out_blk(d), pl.BlockSpec((tm * SUBLANES, LANES), lambda i: (i, 0)),
                   out_blk(LANES), out_blk(LANES), out_blk(LANES),
                   pl.BlockSpec((1, LANES), lambda i: (0, 0))),
        scratch_shapes=[pltpu.VMEM((1, LANES), F32)],
        compiler_params=_cparams(("arbitrary",)),
        name="post",
    )(ha_p, hb_p, x_p, mod_p, mod_p, mod_p, ha_s, hb_s, x_s, mod_s, mod_s, mod_s, norm_ffn, w_out, rw, rb)


def _route_tile(n):
    return max(r for r in range(LANES, 4 * LANES + 1, LANES) if n % r == 0)


def _dest_kernel(idx_ref, rank_ref, start_ref, dest_ref):
    R = idx_ref.shape[0]
    idx = idx_ref[...]
    lane = lax.broadcasted_iota(I32, (R, LANES), 1)
    dest = rank_ref[...]
    for k in range(TOP_K):
        st = jnp.sum(jnp.where(lane == idx[:, k:k + 1], start_ref[...], 0.0), axis=-1, keepdims=True)
        dest = dest + jnp.where(lane == k, st, 0.0)
    dest_ref[...] = dest.T[0:8, :].astype(I32)


def _dest(idx, rank, start_row):
    n = idx.shape[0]
    R = _route_tile(n)
    return pl.pallas_call(
        _dest_kernel,
        out_shape=jax.ShapeDtypeStruct((8, n), I32),
        grid=(n // R,),
        in_specs=[pl.BlockSpec((R, LANES), lambda i: (i, 0)),
                  pl.BlockSpec((R, LANES), lambda i: (i, 0)),
                  pl.BlockSpec((1, LANES), lambda i: (0, 0))],
        out_specs=pl.BlockSpec((8, R), lambda i: (0, i)),
        compiler_params=_cparams(("parallel",)),
        name="dest",
    )(idx, rank, start_row)


def _dispatch_kernel(zb_ref, dest_hbm, x_ref, xs_hbm, dest_smem, zeros, sem_idx, sem_rows, sem_zero):
    R = dest_smem.shape[2]
    S = SUBLANES
    G = DISPATCH_GROUP
    bm = zeros.shape[0]
    i = pl.program_id(0)
    last = pl.num_programs(0) - 1
    slot = lax.rem(i, 2)

    def dest_load(tile, s):
        return pltpu.make_async_copy(dest_hbm.at[:, pl.ds(pl.multiple_of(tile * R, R), R)],
                                     dest_smem.at[s], sem_idx.at[s])

    @pl.when(i == 0)
    def _():
        zeros[...] = jnp.zeros_like(zeros)

        def zero_copy(j):
            return pltpu.make_async_copy(zeros, xs_hbm.at[pl.ds(pl.multiple_of(zb_ref[j] * bm, bm), bm)], sem_zero)

        def start(j, c):
            @pl.when(zb_ref[j] >= 0)
            def _():
                zero_copy(j).start()
            return c

        def wait(j, c):
            @pl.when(zb_ref[j] >= 0)
            def _():
                zero_copy(j).wait()
            return c

        lax.fori_loop(0, zb_ref.shape[0], start, 0)
        lax.fori_loop(0, zb_ref.shape[0], wait, 0)
        dest_load(0, 0).start()

    dest_load(i, slot).wait()

    @pl.when(i < last)
    def _():
        dest_load(i + 1, 1 - slot).start()

    def row_copies(t, g):
        src = x_ref.at[pl.ds(pl.multiple_of(t * S, S), S)]
        return [pltpu.make_async_copy(src, xs_hbm.at[pl.ds(pl.multiple_of(dest_smem[slot, k, t] * S, S), S)],
                                      sem_rows.at[g % 2])
                for k in range(TOP_K)]

    def issue(g):
        def body(t, c):
            for k, cp in enumerate(row_copies(t, g)):
                cp.start(priority=k % 2)
            return c
        lax.fori_loop(g * G, (g + 1) * G, body, 0, unroll=DMA_UNROLL)

    def drain(g):
        def body(t, c):
            for cp in row_copies(t, g):
                cp.wait()
            return c
        lax.fori_loop(g * G, (g + 1) * G, body, 0, unroll=DMA_UNROLL)

    n_groups = R // G
    for g in range(n_groups):
        issue(g)
        if g > 0:
            drain(g - 1)
    drain(n_groups - 1)


def _dispatch(zero_blocks, dest, h2, p_rows):
    S = SUBLANES
    n = h2.shape[0] // S
    R = _route_tile(n)
    assert R % DISPATCH_GROUP == 0
    return pl.pallas_call(
        _dispatch_kernel,
        out_shape=jax.ShapeDtypeStruct((p_rows * S, LANES), F32),
        grid_spec=pltpu.PrefetchScalarGridSpec(
            num_scalar_prefetch=1,
            grid=(n // R,),
            in_specs=[pl.BlockSpec(memory_space=pl.ANY),
                      pl.BlockSpec((R * S, LANES), lambda i, zb: (i, 0))],
            out_specs=pl.BlockSpec(memory_space=pl.ANY),
            scratch_shapes=[pltpu.SMEM((2, 8, R), I32), pltpu.VMEM((EXPERT_BLOCK * S, LANES), F32),
                            pltpu.SemaphoreType.DMA((2,)), pltpu.SemaphoreType.DMA((2,)), pltpu.SemaphoreType.DMA]),
        compiler_params=_cparams(("arbitrary",)),
        name="dispatch",
    )(zero_blocks, dest, h2)


def _expert_kernel(be_ref, nv_ref, x_ref, wgu_ref, bgu_ref, wd_ref, bd_ref, y_ref, wgu_bf, wd_bf):
    i = pl.program_id(0)
    dff = wd_ref.shape[0]

    @pl.when(i < nv_ref[0])
    def _():
        changed = jnp.logical_or(i == 0, be_ref[i] != be_ref[jnp.maximum(i - 1, 0)])

        @pl.when(changed)
        def _():
            rows = 128

            def cast_gu(r, c):
                sl = pl.ds(pl.multiple_of(r * rows, rows), rows)
                wgu_bf[sl, :] = wgu_ref[sl, :].astype(BF16)
                return c

            def cast_d(r, c):
                sl = pl.ds(pl.multiple_of(r * rows, rows), rows)
                wd_bf[sl, :] = wd_ref[sl, :].astype(BF16)
                return c

            lax.fori_loop(0, wgu_ref.shape[0] // rows, cast_gu, 0)
            lax.fori_loop(0, wd_ref.shape[0] // rows, cast_d, 0)

        x = _load_row_tiles(x_ref, x_ref.shape[0] // SUBLANES).astype(BF16)
        g = jnp.minimum(_dot(x, wgu_bf[:, 0:dff]) + bgu_ref[:, 0:dff], SWIGLU_LIMIT)
        u = jnp.clip(_dot(x, wgu_bf[:, dff:2 * dff]) + bgu_ref[:, dff:2 * dff], -SWIGLU_LIMIT, SWIGLU_LIMIT)
        act = (u + 1.0) * (g * _sigmoid(SWIGLU_ALPHA * g))
        _store_row_tiles(y_ref, _dot(act.astype(BF16), wd_bf[...]) + bd_ref[...])

    @pl.when(i >= nv_ref[0])
    def _():
        y_ref[...] = jnp.zeros_like(y_ref)


def _experts(block_e, n_valid, xs, w_gu, b_gu, w_down, b_down):
    S = SUBLANES
    bm = EXPERT_BLOCK
    nb = xs.shape[0] // (bm * S)
    e, d, dff2 = w_gu.shape[1:]
    dff = w_down.shape[2]
    blk = lambda i, be, nv: (jnp.minimum(i, nv[0] - 1), 0)
    return pl.pallas_call(
        _expert_kernel,
        out_shape=jax.ShapeDtypeStruct(xs.shape, F32),
        grid_spec=pltpu.PrefetchScalarGridSpec(
            num_scalar_prefetch=2,
            grid=(nb,),
            in_specs=[pl.BlockSpec((bm * S, LANES), blk),
                      pl.BlockSpec((None, None, d, dff2), lambda i, be, nv: (0, be[i], 0, 0)),
                      pl.BlockSpec((None, 1, dff2), lambda i, be, nv: (be[i], 0, 0)),
                      pl.BlockSpec((None, None, dff, d), lambda i, be, nv: (0, be[i], 0, 0)),
                      pl.BlockSpec((None, 1, d), lambda i, be, nv: (be[i], 0, 0))],
            out_specs=pl.BlockSpec((bm * S, LANES), lambda i, be, nv: (i, 0)),
            scratch_shapes=[pltpu.VMEM((d, dff2), BF16), pltpu.VMEM((dff, d), BF16)]),
        compiler_params=_cparams(("arbitrary",)),
        name="experts",
    )(block_e, n_valid, xs, w_gu, b_gu.reshape(e, 1, dff2), w_down, b_down.reshape(e, 1, d))


def _combine_kernel(dest_hbm, ys_hbm, x1_ref, gate_ref, g2p_ref, g2s_ref, nf_ref,
                    yp_ref, ysm_ref, dest_smem, buf, sem_idx, sem_rows, *, n_prompt_tiles):
    R = x1_ref.shape[0]
    i = pl.program_id(0)
    last = pl.num_programs(0) - 1
    slot = lax.rem(i, 2)

    def dest_load(tile, s):
        return pltpu.make_async_copy(dest_hbm.at[:, pl.ds(pl.multiple_of(tile * R, R), R)],
                                     dest_smem.at[s], sem_idx.at[s])

    S = SUBLANES

    def row_copy(s, src_row, k, t):
        return pltpu.make_async_copy(ys_hbm.at[pl.ds(pl.multiple_of(src_row * S, S), S)],
                                     buf.at[s, k, pl.ds(pl.multiple_of(t * S, S), S)], sem_rows.at[s])

    def gather(s):
        def issue(t, c):
            for k in range(TOP_K):
                row_copy(s, dest_smem[s, k, t], k, t).start(priority=k % 2)
            return c
        lax.fori_loop(0, R, issue, 0, unroll=DMA_UNROLL)

    @pl.when(i == 0)
    def _():
        dest_load(0, 0).start()
        dest_load(0, 0).wait()

        @pl.when(last > 0)
        def _():
            dest_load(1, 1).start()
        gather(0)

    def step(s):
        def drain(t, c):
            for k in range(TOP_K):
                row_copy(s, dest_smem[s, k, t], k, t).wait()
            return c

        lax.fori_loop(0, R, drain, 0, unroll=DMA_UNROLL)

        @pl.when(i < last)
        def _():
            dest_load(i + 1, 1 - s).wait()
            gather(1 - s)

            @pl.when(i + 2 <= last)
            def _():
                dest_load(i + 2, s).start()

        gate = gate_ref[...]
        ff = jnp.zeros(x1_ref.shape, F32)
        for k in range(TOP_K):
            ff = ff + _load_row_tiles(buf, R, lead=(s, k)) * gate[:, k:k + 1]
        is_prompt = i < n_prompt_tiles

        @pl.when(is_prompt)
        def _():
            yp_ref[...] = _rms(x1_ref[...] + g2p_ref[...] * ff, nf_ref[...])

        @pl.when(jnp.logical_not(is_prompt))
        def _():
            ysm_ref[...] = _rms(x1_ref[...] + g2s_ref[...] * ff, nf_ref[...])

    for s in range(2):
        pl.when(slot == s)(functools.partial(step, s))


def _combine(dest, ys, x1, gates, mod_p, mod_s, norm_final, n_prompt, rows_per_batch):
    n, d = x1.shape
    R = ROW_TILE
    npt = n_prompt // R
    tiles = rows_per_batch // R
    nbatch = n_prompt // rows_per_batch
    n_s = n - n_prompt
    return pl.pallas_call(
        functools.partial(_combine_kernel, n_prompt_tiles=npt),
        out_shape=(jax.ShapeDtypeStruct((n_prompt, d), F32), jax.ShapeDtypeStruct((n_s, d), F32)),
        grid=(n // R,),
        in_specs=[pl.BlockSpec(memory_space=pl.ANY),
                  pl.BlockSpec(memory_space=pl.ANY),
                  pl.BlockSpec((R, d), lambda i: (i, 0)),
                  pl.BlockSpec((R, LANES), lambda i: (i, 0)),
                  pl.BlockSpec((None, 1, d), lambda i: (jnp.minimum(i // tiles, nbatch - 1), 0, 5)),
                  pl.BlockSpec((R, d), lambda i: (jnp.maximum(i - npt, 0), 5)),
                  pl.BlockSpec((1, d), lambda i: (0, 0))],
        out_specs=(pl.BlockSpec((R, d), lambda i: (jnp.minimum(i, npt - 1), 0)),
                   pl.BlockSpec((R, d), lambda i: (jnp.maximum(i - npt, 0), 0))),
        scratch_shapes=[pltpu.SMEM((2, 8, R), I32), pltpu.VMEM((2, TOP_K, R * SUBLANES, LANES), F32),
                        pltpu.SemaphoreType.DMA((2,)), pltpu.SemaphoreType.DMA((2,))],
        compiler_params=_cparams(("arbitrary",)),
        name="combine",
    )(dest, ys, x1, gates, mod_p, mod_s, norm_final)


def _reorder_w_in(w):
    W = GROUP_W
    g0 = 4 * W
    g1 = g0 + 2 * N_HEADS
    pad = jnp.zeros((w.shape[0], LANES - 2 * N_HEADS), w.dtype)
    cols = [w[:, 0:g0], w[:, g1:g1 + W], w[:, g1 + 2 * W:g1 + 4 * W], w[:, g1 + W:g1 + 2 * W], w[:, g0:g1], pad]
    return jnp.concatenate(cols, axis=1).astype(BF16)


def kernel(x_prompt, x_sample, c_prompt, c_sample, state_mlstm_C, state_mlstm_n, state_mlstm_m,
           state_hgrn_S, w_ada, b_ada, norm_mix, norm_ffn, w_in, b_gate, norm_a, lb_logits, norm_b,
           w_out, router_w, router_b, w_gu, b_gu, w_down, b_down, norm_final):
    bp, seq, d = x_prompt.shape
    bs = x_sample.shape[0]
    assert x_sample.shape[1] == 1 and w_ada.shape[0] == 1 and d == SUBLANES * LANES
    n_p = bp * seq
    n_all = n_p + bs
    n_exp = router_w.shape[2]
    W = GROUP_W
    n_a = 7 * W

    mod = _ada(jnp.concatenate([c_prompt, c_sample], axis=0), w_ada[0], b_ada)
    mod_p = mod[:bp].reshape(bp, 1, 6 * d)
    mod_s = mod[bp:]

    w_r = _reorder_w_in(w_in[0])
    nmix = norm_mix.reshape(1, d)
    tm = min(TOKEN_TILE, seq)
    xp = x_prompt.reshape(n_p, d)
    xs_ = x_sample.reshape(bs, d)
    pa_p, pb_p = _inproj(xp, mod_p, False, nmix, w_r, n_a, tm, seq, BF16)
    pa_s, pb_s = _inproj(xs_, mod_s, True, nmix, w_r, n_a, bs, None, F32)

    bg_row = jnp.pad(b_gate.reshape(1, 2 * N_HEADS), ((0, 0), (0, LANES - 2 * N_HEADS)))
    na = norm_a.reshape(1, W)
    nb_ = norm_b.reshape(1, W)
    ha_p, c_p, nrm_p, m_p = _mlstm_prompt(pa_p, pb_p, bg_row, na, bp, seq)
    hb_p, s_p = _hgrn_prompt(pa_p, pb_p, lb_logits, nb_, bp, seq)
    ha_s, hb_s, c_s, nrm_s, m_s, s_s = _step_mixers(
        pa_s, pb_s, bg_row, na, nb_, lb_logits, state_mlstm_C, state_mlstm_n, state_mlstm_m, state_hgrn_S)

    w_o = w_out[0].astype(BF16)
    rw = jnp.pad(router_w[0], ((0, 0), (0, LANES - n_exp))).astype(BF16)
    rb = jnp.pad(router_b.reshape(1, n_exp), ((0, 0), (0, LANES - n_exp)), constant_values=NEG)
    nffn = norm_ffn.reshape(1, d)
    x1, h2, idx, gates, rank, counts = _post(ha_p, hb_p, xp, mod_p, ha_s, hb_s, xs_, mod_s, nffn, w_o, rw, rb,
                                             tm, seq)

    bm = EXPERT_BLOCK
    cnt = counts[0].astype(I32)
    padded = ((cnt + bm - 1) // bm) * bm
    ends = jnp.cumsum(padded)
    start_row = (ends - padded).astype(F32).reshape(1, LANES)
    dest = _dest(idx, rank, start_row)
    n_blocks = (n_all * TOP_K + n_exp * (bm - 1) + bm - 1) // bm
    n_valid = (ends[n_exp - 1] // bm).astype(I32)
    blk_start = jnp.arange(n_blocks, dtype=I32) * bm
    block_e = jnp.sum((ends[None, :n_exp] <= blk_start[:, None]).astype(I32), axis=1)
    block_e = jnp.minimum(block_e, n_exp - 1)
    last_e = block_e[jnp.maximum(n_valid - 1, 0)]
    block_e = jnp.where(jnp.arange(n_blocks) < n_valid, block_e, last_e)

    min_valid = -(-(n_all * TOP_K) // bm)
    group_last = jnp.where(padded[:n_exp] > 0, ends[:n_exp] // bm - 1, -1)
    tail = n_valid + jnp.arange(n_blocks - min_valid, dtype=I32)
    tail = jnp.where(tail < n_blocks, tail, -1)
    zero_blocks = jnp.concatenate([group_last.astype(I32), tail])

    xs_sorted = _dispatch(zero_blocks, dest, h2, n_blocks * bm)
    ys_sorted = _experts(block_e, n_valid.reshape(1), xs_sorted, w_gu, b_gu[0], w_down, b_down[0])
    y_p, y_s = _combine(dest, ys_sorted, x1, gates, mod_p, mod_s, norm_final.reshape(1, d), n_p, seq)

    m_p_out = m_p[:, :N_HEADS, 0][None]
    return (y_p.reshape(bp, seq, d), y_s.reshape(bs, 1, d), c_p, nrm_p, m_p_out, s_p,
            c_s, nrm_s, m_s, s_s)
```

```python
import functools

import jax
import jax.numpy as jnp
from jax import lax
from jax.experimental import pallas as pl
from jax.experimental.pallas import tpu as pltpu

F32 = jnp.float32
BF16 = jnp.bfloat16
I32 = jnp.int32

EPS = 1e-6
NEG = -1e30
SWIGLU_LIMIT = 7.0
SWIGLU_ALPHA = 1.702
TOP_K = 4

LANES = 128
HEAD_DIM = 128
N_HEADS = 4
GROUP_W = N_HEADS * HEAD_DIM
VMEM_LIMIT = 56 * 1024 * 1024

MLSTM_CHUNK = 256
HGRN_BLOCK = 128
HGRN_SUB = 16
TOKEN_TILE = 512
ROW_TILE = 128
EXPERT_BLOCK = 512
STEP_BATCH = 8
DISPATCH_GROUP = 64
DMA_UNROLL = 8
DEST_ALIGN = 1024
STEP_ROWS = 32


def _cparams(sem, vmem=VMEM_LIMIT):
    return pltpu.CompilerParams(dimension_semantics=sem, vmem_limit_bytes=vmem)


def _dot(a, b):
    return jnp.dot(a, b, preferred_element_type=F32)


def _dot_nt(a, b):
    return lax.dot_general(a, b, (((1,), (1,)), ((), ())), preferred_element_type=F32)


def _dot_tn(a, b):
    return lax.dot_general(a, b, (((0,), (0,)), ((), ())), preferred_element_type=F32)


def _sigmoid(x):
    return 1.0 / (1.0 + jnp.exp(-x))


def _log_sigmoid(x):
    return jnp.minimum(x, 0.0) - jnp.log1p(jnp.exp(-jnp.abs(x)))


def _rms(x, g):
    return x * lax.rsqrt(jnp.mean(x * x, axis=-1, keepdims=True) + EPS) * g


def _cumsum_rows(tri, x):
    hi = x.astype(BF16)
    r1 = x - hi.astype(F32)
    mid = r1.astype(BF16)
    lo = (r1 - mid.astype(F32)).astype(BF16)
    return _dot(tri, hi) + _dot(tri, mid) + _dot(tri, lo)


SUBLANES = 8


def _store_row_tiles(ref, x):
    rows = x.shape[0]
    for c in range(SUBLANES):
        ref[pl.ds(c, rows, stride=SUBLANES), :] = x[:, c * LANES:(c + 1) * LANES]


def _load_row_tiles(ref, rows, lead=()):
    return jnp.concatenate([ref[lead + (pl.ds(c, rows, stride=SUBLANES), slice(None))]
                            for c in range(SUBLANES)], axis=-1)


def _ada_kernel(c_ref, w_ref, b_ref, o_ref):
    c = c_ref[...]
    a = (c * _sigmoid(c)).astype(BF16)
    o_ref[...] = _dot(a, w_ref[...].astype(BF16)) + b_ref[...]


def _ada(c_all, w, b):
    m, d = c_all.shape
    n = w.shape[1]
    tn = 1024
    return pl.pallas_call(
        _ada_kernel,
        out_shape=jax.ShapeDtypeStruct((m, n), F32),
        grid=(n // tn,),
        in_specs=[pl.BlockSpec((m, d), lambda j: (0, 0)),
                  pl.BlockSpec((d, tn), lambda j: (0, j)),
                  pl.BlockSpec((1, tn), lambda j: (0, j))],
        out_specs=pl.BlockSpec((m, tn), lambda j: (0, j)),
        compiler_params=_cparams(("parallel",)),
        name="ada",
    )(c_all, w, b)


def _inproj_kernel(x_ref, sh_ref, sc_ref, nw_ref, w_ref, oa_ref, ob_ref):
    h = _rms(x_ref[...], nw_ref[...]) * (1.0 + sc_ref[...]) + sh_ref[...]
    hb = h.astype(BF16)
    na = oa_ref.shape[1]
    nb = ob_ref.shape[1]
    for j in range(0, na, GROUP_W):
        oa_ref[:, j:j + GROUP_W] = _dot(hb, w_ref[:, j:j + GROUP_W]).astype(oa_ref.dtype)
    ob_ref[:, 0:GROUP_W] = _dot(hb, w_ref[:, na:na + GROUP_W])
    ob_ref[:, GROUP_W:nb] = _dot(hb, w_ref[:, na + GROUP_W:na + nb])


def _inproj(x, mod_rows, mod_is_per_row, norm_w, w_r, n_a, tm, rows_per_batch, out_dtype):
    n, d = x.shape
    n_b = w_r.shape[1] - n_a
    if mod_is_per_row:
        sh_spec = pl.BlockSpec((tm, d), lambda i: (i, 0))
        sc_spec = pl.BlockSpec((tm, d), lambda i: (i, 1))
    else:
        tiles = rows_per_batch // tm
        sh_spec = pl.BlockSpec((None, 1, d), lambda i: (i // tiles, 0, 0))
        sc_spec = pl.BlockSpec((None, 1, d), lambda i: (i // tiles, 0, 1))
    return pl.pallas_call(
        _inproj_kernel,
        out_shape=(jax.ShapeDtypeStruct((n, n_a), out_dtype),
                   jax.ShapeDtypeStruct((n, n_b), F32)),
        grid=(n // tm,),
        in_specs=[pl.BlockSpec((tm, d), lambda i: (i, 0)), sh_spec, sc_spec,
                  pl.BlockSpec((1, d), lambda i: (0, 0)),
                  pl.BlockSpec(w_r.shape, lambda i: (0, 0))],
        out_specs=(pl.BlockSpec((tm, n_a), lambda i: (i, 0)),
                   pl.BlockSpec((tm, n_b), lambda i: (i, 0))),
        compiler_params=_cparams(("parallel",)),
        name="inproj",
    )(x, mod_rows, mod_rows, norm_w, w_r)


def _mlstm_kernel(q_ref, k_ref, v_ref, o_ref, g_ref, bg_ref, na_ref,
                  h_ref, c_ref, n_ref, m_ref):
    L = q_ref.shape[0]
    scale = HEAD_DIM ** -0.5

    @pl.when(pl.program_id(1) == 0)
    def _():
        c_ref[...] = jnp.zeros_like(c_ref)
        n_ref[...] = jnp.zeros_like(n_ref)
        m_ref[...] = jnp.full_like(m_ref, NEG)

    g = g_ref[...] + bg_ref[...]
    lane = lax.broadcasted_iota(I32, (L, LANES), 1)
    gates = jnp.where(lane < N_HEADS, g, _log_sigmoid(g))
    row = lax.broadcasted_iota(I32, (L, L), 0)
    col = lax.broadcasted_iota(I32, (L, L), 1)
    causal = row >= col
    tri = causal.astype(BF16)
    csum = _cumsum_rows(tri, gates)
    gates_t = gates.T
    csum_t = csum.T

    for h in range(N_HEADS):
        sl = slice(h * HEAD_DIM, (h + 1) * HEAD_DIM)
        qh, kh, vh = q_ref[:, sl], k_ref[:, sl], v_ref[:, sl]
        b_col = csum[:, N_HEADS + h:N_HEADS + h + 1]
        li_col = gates[:, h:h + 1]
        b_row = csum_t[N_HEADS + h:N_HEADS + h + 1, :]
        li_row = gates_t[h:h + 1, :]
        m_prev = m_ref[h:h + 1, 0:1]
        c_prev = c_ref[h]
        n_prev = n_ref[h:h + 1, :]

        dm = jnp.where(causal, b_col - b_row + li_row, NEG)
        inter = b_col + m_prev
        m_t = jnp.maximum(inter, jnp.max(dm, axis=-1, keepdims=True))
        w = jnp.exp(dm - m_t) * (_dot_nt(qh, kh) * scale)
        wi = jnp.exp(inter - m_t)
        num = _dot(w.astype(BF16), vh) + wi * _dot(qh, c_prev.astype(BF16))
        den = (jnp.sum(w, axis=-1, keepdims=True)
               + wi * jnp.sum(qh.astype(F32) * n_prev, axis=-1, keepdims=True))
        hval = num / jnp.maximum(jnp.abs(den), jnp.exp(-m_t))

        m_new = m_t[L - 1:L, :]
        b_last = b_col[L - 1:L, :]
        ws = jnp.exp(b_last - b_col + li_col - m_new)
        dec = jnp.exp(b_last + m_prev - m_new)
        kw = kh.astype(F32) * (ws * scale)
        c_ref[h] = dec * c_prev + _dot_tn(kw.astype(BF16), vh)
        n_ref[h:h + 1, :] = dec * n_prev + jnp.sum(kw, axis=0, keepdims=True)
        m_ref[h:h + 1, :] = jnp.broadcast_to(m_new, (1, LANES))

        hn = _rms(hval, na_ref[:, sl]) * _sigmoid(o_ref[:, sl].astype(F32))
        h_ref[:, sl] = hn.astype(h_ref.dtype)


def _mlstm_prompt(pa, pb, b_gate_row, norm_a, bsz, seq):
    L = MLSTM_CHUNK
    nc = seq // L
    n = bsz * seq
    gate_blk = (pb.shape[1] - LANES) // LANES

    def col(j):
        return pl.BlockSpec((L, GROUP_W), lambda b, c: (b * nc + c, j))

    return pl.pallas_call(
        _mlstm_kernel,
        out_shape=(jax.ShapeDtypeStruct((n, GROUP_W), BF16),
                   jax.ShapeDtypeStruct((1, bsz, N_HEADS, HEAD_DIM, HEAD_DIM), F32),
                   jax.ShapeDtypeStruct((1, bsz, N_HEADS, HEAD_DIM), F32),
                   jax.ShapeDtypeStruct((bsz, 8, LANES), F32)),
        grid=(bsz, nc),
        in_specs=[col(0), col(1), col(2), col(3),
                  pl.BlockSpec((L, LANES), lambda b, c: (b * nc + c, gate_blk)),
                  pl.BlockSpec((1, LANES), lambda b, c: (0, 0)),
                  pl.BlockSpec((1, GROUP_W), lambda b, c: (0, 0))],
        out_specs=(pl.BlockSpec((L, GROUP_W), lambda b, c: (b * nc + c, 0)),
                   pl.BlockSpec((None, None, N_HEADS, HEAD_DIM, HEAD_DIM), lambda b, c: (0, b, 0, 0, 0)),
                   pl.BlockSpec((None, None, N_HEADS, HEAD_DIM), lambda b, c: (0, b, 0, 0)),
                   pl.BlockSpec((None, 8, LANES), lambda b, c: (b, 0, 0))),
        compiler_params=_cparams(("parallel", "arbitrary")),
        name="mlstm_prompt",
    )(pa, pa, pa, pa, pb, b_gate_row, norm_a)


def _lower_bound(lb_logits_ref):
    lg = lb_logits_ref[...]
    e = jnp.exp(lg - jnp.max(lg, axis=0, keepdims=True))
    return e[0:1, :] / jnp.sum(e, axis=0, keepdims=True)


def _hgrn_kernel(q_ref, v_ref, g_ref, f_ref, lbl_ref, nb_ref, h_ref, s_ref, st_ref):
    LB = q_ref.shape[0]
    C = HGRN_SUB
    NS = LB // C
    H2 = C // 2
    assert LB == LANES
    scale = HEAD_DIM ** -0.5

    @pl.when(pl.program_id(1) == 0)
    def _():
        st_ref[...] = jnp.zeros_like(st_ref)

    lb = _lower_bound(lbl_ref)
    f = lb + (1.0 - lb) * _sigmoid(f_ref[...])
    qraw = q_ref[...].astype(F32)
    q_all = qraw * _sigmoid(qraw) * scale
    k_all = 1.0 - f
    row = lax.broadcasted_iota(I32, (LB, LB), 0)
    col = lax.broadcasted_iota(I32, (LB, LB), 1)
    b_all = _cumsum_rows((row >= col).astype(BF16), jnp.log(f))

    rl = lax.broadcasted_iota(I32, (C, LANES), 0)
    cl = lax.broadcasted_iota(I32, (C, LANES), 1)
    ones_bf = jnp.ones((HEAD_DIM, LANES), BF16)

    for h in range(N_HEADS):
        sl = slice(h * HEAD_DIM, (h + 1) * HEAD_DIM)
        qh, kh, bh = q_all[:, sl], k_all[:, sl], b_all[:, sl]
        vh = v_ref[:, sl]
        st = st_ref[h]

        parts = []
        for i in range(NS):
            q_i, k_i, b_i = (x[i * C:(i + 1) * C, :] for x in (qh, kh, bh))
            for s in range(C):
                lo = 0 if s < H2 else H2
                parts.append(q_i[lo:, :] * k_i[s:s + 1, :]
                             * jnp.exp(jnp.minimum(b_i[lo:, :] - b_i[s:s + 1, :], 0.0)))
        sums = _dot(jnp.concatenate(parts, axis=0).astype(BF16), ones_bf)

        a_rows = []
        off = 0
        for i in range(NS):
            a = jnp.zeros((C, LANES), F32)
            for s in range(C):
                lo = 0 if s < H2 else H2
                blk = sums[off:off + C - lo, :]
                if lo:
                    blk = jnp.concatenate([jnp.zeros((lo, LANES), F32), blk], axis=0)
                a = jnp.where(cl == i * C + s, blk, a)
                off += C - lo
            a = jnp.where(rl + i * C >= cl, a, 0.0)
            if i > 0:
                b_i = bh[i * C:(i + 1) * C, :]
                r_i = b_i[0:1, :]
                qs = qh[i * C:(i + 1) * C, :] * jnp.exp(b_i - r_i)
                ks = kh[0:i * C, :] * jnp.exp(jnp.minimum(r_i - bh[0:i * C, :], 0.0))
                ks = jnp.concatenate([ks, jnp.zeros((LB - i * C, HEAD_DIM), F32)], axis=0)
                a = a + _dot_nt(qs.astype(BF16), ks.astype(BF16))
            a_rows.append(a)
        a_full = jnp.concatenate(a_rows, axis=0)

        o = _dot(a_full.astype(BF16), vh) + _dot_nt((qh * jnp.exp(bh)).astype(BF16), st.astype(BF16))
        b_l = bh[LB - 1:LB, :]
        kd = kh * jnp.exp(b_l - bh)
        st_ref[h] = st * jnp.exp(b_l) + _dot_tn(vh, kd.astype(BF16))

        gv = g_ref[:, sl].astype(F32)
        hn = _rms(o, nb_ref[:, sl]) * (gv * _sigmoid(gv))
        h_ref[:, sl] = hn.astype(h_ref.dtype)

    @pl.when(pl.program_id(1) == pl.num_programs(1) - 1)
    def _():
        for h in range(N_HEADS):
            s_ref[h] = st_ref[h].T


def _hgrn_prompt(pa, pb, lb_logits, norm_b, bsz, seq):
    LB = HGRN_BLOCK
    nc = seq // LB
    n = bsz * seq

    def col(j):
        return pl.BlockSpec((LB, GROUP_W), lambda b, c: (b * nc + c, j))

    return pl.pallas_call(
        _hgrn_kernel,
        out_shape=(jax.ShapeDtypeStruct((n, GROUP_W), BF16),
                   jax.ShapeDtypeStruct((1, bsz, N_HEADS, HEAD_DIM, HEAD_DIM), F32)),
        grid=(bsz, nc),
        in_specs=[col(4), col(5), col(6),
                  pl.BlockSpec((LB, GROUP_W), lambda b, c: (b * nc + c, 0)),
                  pl.BlockSpec(lb_logits.shape, lambda b, c: (0, 0)),
                  pl.BlockSpec((1, GROUP_W), lambda b, c: (0, 0))],
        out_specs=(pl.BlockSpec((LB, GROUP_W), lambda b, c: (b * nc + c, 0)),
                   pl.BlockSpec((None, None, N_HEADS, HEAD_DIM, HEAD_DIM), lambda b, c: (0, b, 0, 0, 0))),
        scratch_shapes=[pltpu.VMEM((N_HEADS, HEAD_DIM, HEAD_DIM), F32)],
        compiler_params=_cparams(("parallel", "arbitrary")),
        name="hgrn_prompt",
    )(pa, pa, pa, pb, lb_logits, norm_b)


def _step_kernel(pa_ref, pb_ref, bg_ref, na_ref, nb_ref, lbl_ref,
                 c0_ref, n0_ref, m0_ref, s0_ref,
                 ha_ref, hb_ref, c1_ref, n1_ref, m1_ref, s1_ref, a_ref, b_ref, q_ref):
    scale = HEAD_DIM ** -0.5
    W = GROUP_W
    H = N_HEADS
    D = HEAD_DIM
    CB = 3 * D
    lb = _lower_bound(lbl_ref)
    gates_all = pb_ref[:, W:W + LANES] + bg_ref[...]
    f_all = lb + (1.0 - lb) * _sigmoid(pb_ref[:, 0:W])
    a_ref[...] = jnp.zeros_like(a_ref)
    b_ref[...] = jnp.zeros_like(b_ref)
    q_ref[...] = jnp.zeros_like(q_ref)
    for j in range(STEP_BATCH):
        for h in range(H):
            for part in range(3):
                b_ref[j, (2 + part) * H + h:(2 + part) * H + h + 1, h * CB + 2 * D:h * CB + 3 * D] = (
                    jnp.ones((1, D), F32))

    for j in range(STEP_BATCH):
        row = slice(j, j + 1)
        aux = []
        for h in range(H):
            q = pa_ref[row, h * D:(h + 1) * D]
            k = pa_ref[row, W + h * D:W + (h + 1) * D]
            v = pa_ref[row, 2 * W + h * D:2 * W + (h + 1) * D]
            li = gates_all[row, h:h + 1]
            lf = _log_sigmoid(gates_all[row, H + h:H + h + 1])
            inter = lf + m0_ref[row, h:h + 1]
            m_t = jnp.maximum(inter, li)
            ws = jnp.exp(li - m_t)
            dec = jnp.exp(inter - m_t)
            kw = k * (ws * scale)

            qraw = pa_ref[row, 4 * W + h * D:4 * W + (h + 1) * D]
            qb = qraw * _sigmoid(qraw) * scale
            vb = pa_ref[row, 5 * W + h * D:5 * W + (h + 1) * D]
            f = f_all[row, h * D:(h + 1) * D]
            decay = jnp.exp(jnp.log(f))
            kb = 1.0 - f
            d_hi = decay.astype(BF16).astype(F32)
            d_mid = (decay - d_hi).astype(BF16).astype(F32)

            a_ref[j, h:h + 1, :] = kw
            a_ref[j, H + h:H + h + 1, :] = kb
            a_ref[j, 2 * H + h:2 * H + h + 1, :] = d_hi
            a_ref[j, 3 * H + h:3 * H + h + 1, :] = d_mid
            a_ref[j, 4 * H + h:4 * H + h + 1, :] = decay - d_hi - d_mid
            b_ref[j, h:h + 1, h * CB:h * CB + D] = v
            b_ref[j, H + h:H + h + 1, h * CB + D:h * CB + 2 * D] = vb
            q_ref[j, h:h + 1, :] = q
            q_ref[j, H + h:H + h + 1, :] = qb * decay
            aux.append((q, k, v, kw, m_t, ws, dec, qb, kb, vb))

        upd = _dot_tn(a_ref[j].astype(BF16), b_ref[j].astype(BF16))
        q_rows = q_ref[j].astype(BF16)
        for h in range(H):
            sl = slice(h * D, (h + 1) * D)
            q, k, v, kw, m_t, ws, dec, qb, kb, vb = aux[h]
            og = pa_ref[row, 3 * W + h * D:3 * W + (h + 1) * D]
            c0 = c0_ref[j, h]
            n0 = n0_ref[j, h:h + 1, :]
            w = ws * (jnp.sum(q * k, axis=-1, keepdims=True) * scale)
            num = w * v + dec * _dot(q_rows, c0.astype(BF16))[h:h + 1, :]
            den = w + dec * jnp.sum(q * n0, axis=-1, keepdims=True)
            hval = num / jnp.maximum(jnp.abs(den), jnp.exp(-m_t))
            c1_ref[j, h] = dec * c0 + upd[:, h * CB:h * CB + D]
            n1_ref[j, h:h + 1, :] = dec * n0 + kw
            m1_ref[row, h:h + 1] = m_t
            ha_ref[row, sl] = _rms(hval, na_ref[:, sl]) * _sigmoid(og)
            gv = pa_ref[row, 6 * W + h * D:6 * W + (h + 1) * D]
            s0 = s0_ref[j, h]
            a = jnp.sum(qb * kb, axis=-1, keepdims=True)
            o = a * vb + _dot(q_rows, s0.astype(BF16))[H + h:H + h + 1, :]
            s1_ref[j, h] = upd[:, h * CB + 2 * D:h * CB + 3 * D] * s0 + upd[:, h * CB + D:h * CB + 2 * D]
            hb_ref[row, sl] = _rms(o, nb_ref[:, sl]) * (gv * _sigmoid(gv))


def _step_mixers(pa, pb, b_gate_row, norm_a, norm_b, lb_logits, c0, n0, m0, s0):
    bs = pa.shape[0]
    sb = STEP_BATCH
    st5 = pl.BlockSpec((None, sb, N_HEADS, HEAD_DIM, HEAD_DIM), lambda i: (0, i, 0, 0, 0))
    st4 = pl.BlockSpec((None, sb, N_HEADS, HEAD_DIM), lambda i: (0, i, 0, 0))
    st3 = pl.BlockSpec((None, sb, N_HEADS), lambda i: (0, i, 0))
    rowblk = lambda w: pl.BlockSpec((sb, w), lambda i: (i, 0))
    const = lambda a: pl.BlockSpec(a.shape, lambda i: (0,) * a.ndim)
    return pl.pallas_call(
        _step_kernel,
        out_shape=(jax.ShapeDtypeStruct((bs, GROUP_W), F32),
                   jax.ShapeDtypeStruct((bs, GROUP_W), F32),
                   jax.ShapeDtypeStruct(c0.shape, F32),
                   jax.ShapeDtypeStruct(n0.shape, F32),
                   jax.ShapeDtypeStruct(m0.shape, F32),
                   jax.ShapeDtypeStruct(s0.shape, F32)),
        grid=(bs // sb,),
        in_specs=[rowblk(pa.shape[1]), rowblk(pb.shape[1]), const(b_gate_row), const(norm_a),
                  const(norm_b), const(lb_logits), st5, st4, st3, st5],
        out_specs=(rowblk(GROUP_W), rowblk(GROUP_W), st5, st4, st3, st5),
        scratch_shapes=[pltpu.VMEM((sb, STEP_ROWS, HEAD_DIM), F32),
                        pltpu.VMEM((sb, STEP_ROWS, 3 * N_HEADS * HEAD_DIM), F32),
                        pltpu.VMEM((sb, 16, HEAD_DIM), F32)],
        compiler_params=_cparams(("parallel",)),
        name="step_mixers",
    )(pa, pb, b_gate_row, norm_a, norm_b, lb_logits, c0, n0, m0, s0)


def _post_rows(ha_ref, hb_ref, x_ref, g1_ref, sh2_ref, sc2_ref, nf_ref, wo_ref, rw_ref, rb_ref,
               x1_ref, h2_ref, idx_ref, gate_ref, rank_ref, cnt_ref, base_ref):
    tm = x_ref.shape[0]
    mix = (_dot(ha_ref[...].astype(BF16), wo_ref[0:GROUP_W, :])
           + _dot(hb_ref[...].astype(BF16), wo_ref[GROUP_W:2 * GROUP_W, :]))
    x1 = x_ref[...] + g1_ref[...] * mix
    x1_ref[0:tm, :] = x1
    h2 = _rms(x1, nf_ref[...]) * (1.0 + sc2_ref[...]) + sh2_ref[...]
    _store_row_tiles(h2_ref, h2)
    logits = _dot(h2.astype(BF16), rw_ref[...]) + rb_ref[...]
    lane = lax.broadcasted_iota(I32, (tm, LANES), 1)
    lane_f = lane.astype(F32)
    cur = logits
    vals, ids = [], []
    for _ in range(TOP_K):
        mx = jnp.max(cur, axis=-1, keepdims=True)
        am = jnp.min(jnp.where(cur == mx, lane_f, float(LANES)), axis=-1, keepdims=True)
        vals.append(mx)
        ids.append(am)
        cur = jnp.where(lane_f == am, -jnp.inf, cur)
    es = [jnp.exp(v - vals[0]) for v in vals]
    tot = es[0] + es[1] + es[2] + es[3]
    idx_out = jnp.full((tm, LANES), -1.0, F32)
    gate_out = jnp.zeros((tm, LANES), F32)
    for k in range(TOP_K):
        idx_out = jnp.where(lane == k, ids[k], idx_out)
        gate_out = jnp.where(lane == k, es[k] / tot, gate_out)
    idx_ref[0:tm, :] = idx_out.astype(I32)
    gate_ref[0:tm, :] = gate_out

    hits = [lane_f == ids[k] for k in range(TOP_K)]
    onehot = jnp.zeros((tm, LANES), F32)
    for hk in hits:
        onehot = onehot + hk.astype(F32)
    row = lax.broadcasted_iota(I32, (tm, tm), 0)
    col = lax.broadcasted_iota(I32, (tm, tm), 1)
    before = _dot((row > col).astype(BF16), onehot.astype(BF16)) + base_ref[...]
    rank = jnp.zeros((tm, LANES), F32)
    for k, hk in enumerate(hits):
        rank = jnp.where(lane == k, jnp.sum(jnp.where(hk, before, 0.0), axis=-1, keepdims=True), rank)
    rank_ref[0:tm, :] = rank
    base = base_ref[...] + jnp.sum(onehot, axis=0, keepdims=True)
    base_ref[...] = base
    cnt_ref[...] = base


def _post_kernel(ha_ref, hb_ref, x_ref, g1_ref, sh2_ref, sc2_ref,
                 has_ref, hbs_ref, xs_ref, g1s_ref, sh2s_ref, sc2s_ref,
                 nf_ref, wo_ref, rw_ref, rb_ref, x1_ref, h2_ref, idx_ref, gate_ref, rank_ref, cnt_ref,
                 base_ref, *, n_prompt_tiles):
    i = pl.program_id(0)
    shared = (nf_ref, wo_ref, rw_ref, rb_ref, x1_ref, h2_ref, idx_ref, gate_ref, rank_ref, cnt_ref, base_ref)

    @pl.when(i == 0)
    def _():
        base_ref[...] = jnp.zeros_like(base_ref)

    @pl.when(i < n_prompt_tiles)
    def _():
        _post_rows(ha_ref, hb_ref, x_ref, g1_ref, sh2_ref, sc2_ref, *shared)

    @pl.when(i == n_prompt_tiles)
    def _():
        _post_rows(has_ref, hbs_ref, xs_ref, g1s_ref, sh2s_ref, sc2s_ref, *shared)


def _post(ha_p, hb_p, x_p, mod_p, ha_s, hb_s, x_s, mod_s, norm_ffn, w_out, rw, rb, tm, rows_per_batch):
    n_p, d = x_p.shape
    n_s = x_s.shape[0]
    assert n_s <= tm
    npt = n_p // tm
    tiles = rows_per_batch // tm
    nbatch = n_p // rows_per_batch
    n_all = n_p + n_s
    prow = lambda w: pl.BlockSpec((tm, w), lambda i: (jnp.minimum(i, npt - 1), 0))
    pmod = lambda j: pl.BlockSpec((None, 1, d), lambda i: (jnp.minimum(i // tiles, nbatch - 1), 0, j))
    smod = lambda j: pl.BlockSpec((n_s, d), lambda i: (0, j))
    const = lambda a: pl.BlockSpec(a.shape, lambda i: (0,) * a.ndim)
    out_blk = lambda w: pl.BlockSpec((tm, w), lambda i: (i, 0))
    return pl.pallas_call(
        functools.partial(_post_kernel, n_prompt_tiles=npt),
        out_shape=(jax.ShapeDtypeStruct((n_all, d), F32),
                   jax.ShapeDtypeStruct((n_all * SUBLANES, LANES), F32),
                   jax.ShapeDtypeStruct((n_all, LANES), I32),
                   jax.ShapeDtypeStruct((n_all, LANES), F32),
                   jax.ShapeDtypeStruct((n_all, LANES), F32),
                   jax.ShapeDtypeStruct((1, LANES), F32)),
        grid=(npt + 1,),
        in_specs=[prow(GROUP_W), prow(GROUP_W), prow(d), pmod(2), pmod(3), pmod(4),
                  const(ha_s), const(hb_s), const(x_s), smod(2), smod(3), smod(4),
                  const(norm_ffn), const(w_out), const(rw), const(rb)],
        out_specs=(out_blk(d), pl.BlockSpec((tm * SUBLANES, LANES), lambda i: (i, 0)),
                   out_blk(LANES), out_blk(LANES), out_blk(LANES),
                   pl.BlockSpec((1, LANES), lambda i: (0, 0))),
        scratch_shapes=[pltpu.VMEM((1, LANES), F32)],
        compiler_params=_cparams(("arbitrary",)),
        name="post",
    )(ha_p, hb_p, x_p, mod_p, mod_p, mod_p, ha_s, hb_s, x_s, mod_s, mod_s, mod_s, norm_ffn, w_out, rw, rb)


def _route_tile(n):
    return max(r for r in range(LANES, 4 * LANES + 1, LANES) if n % r == 0)


def _dest_kernel(idx_ref, rank_ref, start_ref, dest_ref):
    R = idx_ref.shape[0]
    idx = idx_ref[...]
    lane = lax.broadcasted_iota(I32, (R, LANES), 1)
    dest = rank_ref[...]
    for k in range(TOP_K):
        st = jnp.sum(jnp.where(lane == idx[:, k:k + 1], start_ref[...], 0.0), axis=-1, keepdims=True)
        dest = dest + jnp.where(lane == k, st, 0.0)
    dest_ref[...] = dest.T[0:8, :].astype(I32)


def _dest(idx, rank, start_row):
    n = idx.shape[0]
    R = _route_tile(n)
    return pl.pallas_call(
        _dest_kernel,
        out_shape=jax.ShapeDtypeStruct((8, n), I32),
        grid=(n // R,),
        in_specs=[pl.BlockSpec((R, LANES), lambda i: (i, 0)),
                  pl.BlockSpec((R, LANES), lambda i: (i, 0)),
                  pl.BlockSpec((1, LANES), lambda i: (0, 0))],
        out_specs=pl.BlockSpec((8, R), lambda i: (0, i)),
        compiler_params=_cparams(("parallel",)),
        name="dest",
    )(idx, rank, start_row)


def _dispatch_kernel(zb_ref, dest_hbm, x_ref, xs_hbm, dest_smem, zeros, sem_idx, sem_rows, sem_zero):
    S = SUBLANES
    R = x_ref.shape[0] // S
    G = DISPATCH_GROUP
    rec = dest_smem.shape[0] // 2
    bm = zeros.shape[0]
    i = pl.program_id(0)
    last = pl.num_programs(0) - 1
    slot = lax.rem(i, 2)

    def dest_load(tile, s):
        return pltpu.make_async_copy(dest_hbm.at[pl.ds(pl.multiple_of(tile * rec, rec), rec)],
                                     dest_smem.at[pl.ds(s * rec, rec)], sem_idx.at[s])

    @pl.when(i == 0)
    def _():
        zeros[...] = jnp.zeros_like(zeros)

        def zero_copy(j):
            return pltpu.make_async_copy(zeros, xs_hbm.at[pl.ds(pl.multiple_of(zb_ref[j] * bm, bm), bm)], sem_zero)

        def start(j, c):
            @pl.when(zb_ref[j] >= 0)
            def _():
                zero_copy(j).start()
            return c

        def wait(j, c):
            @pl.when(zb_ref[j] >= 0)
            def _():
                zero_copy(j).wait()
            return c

        lax.fori_loop(0, zb_ref.shape[0], start, 0)
        lax.fori_loop(0, zb_ref.shape[0], wait, 0)
        dest_load(0, 0).start()

    def step(s):
        dest_load(i, s).wait()

        @pl.when(i < last)
        def _():
            dest_load(i + 1, 1 - s).start()

        def row_copies(t, g):
            src = x_ref.at[pl.ds(pl.multiple_of(t * S, S), S)]
            return [pltpu.make_async_copy(
                        src, xs_hbm.at[pl.ds(pl.multiple_of(dest_smem[s * rec + k * R + t] * S, S), S)],
                        sem_rows.at[g % 2])
                    for k in range(TOP_K)]

        def issue(g):
            def body(t, c):
                for k, cp in enumerate(row_copies(t, g)):
                    cp.start(priority=k % 2)
                return c
            lax.fori_loop(g * G, (g + 1) * G, body, 0, unroll=DMA_UNROLL)

        def drain(g):
            def body(t, c):
                for cp in row_copies(t, g):
                    cp.wait()
                return c
            lax.fori_loop(g * G, (g + 1) * G, body, 0, unroll=DMA_UNROLL)

        n_groups = R // G
        for g in range(n_groups):
            issue(g)
            if g > 0:
                drain(g - 1)
        drain(n_groups - 1)

    for s in range(2):
        pl.when(slot == s)(functools.partial(step, s))


def _dest_records(dest, R):
    n = dest.shape[1]
    rec = -(-TOP_K * R // DEST_ALIGN) * DEST_ALIGN
    d = dest[:TOP_K].reshape(TOP_K, n // R, R).transpose(1, 0, 2).reshape(n // R, TOP_K * R)
    return jnp.pad(d, ((0, 0), (0, rec - TOP_K * R))).reshape(-1), rec


def _dispatch(zero_blocks, dest, h2, p_rows):
    S = SUBLANES
    n = h2.shape[0] // S
    R = _route_tile(n)
    assert R % DISPATCH_GROUP == 0
    dest, rec = _dest_records(dest, R)
    return pl.pallas_call(
        _dispatch_kernel,
        out_shape=jax.ShapeDtypeStruct((p_rows * S, LANES), F32),
        grid_spec=pltpu.PrefetchScalarGridSpec(
            num_scalar_prefetch=1,
            grid=(n // R,),
            in_specs=[pl.BlockSpec(memory_space=pl.ANY),
                      pl.BlockSpec((R * S, LANES), lambda i, zb: (i, 0))],
            out_specs=pl.BlockSpec(memory_space=pl.ANY),
            scratch_shapes=[pltpu.SMEM((2 * rec,), I32), pltpu.VMEM((EXPERT_BLOCK * S, LANES), F32),
                            pltpu.SemaphoreType.DMA((2,)), pltpu.SemaphoreType.DMA((2,)), pltpu.SemaphoreType.DMA]),
        compiler_params=_cparams(("arbitrary",)),
        name="dispatch",
    )(zero_blocks, dest, h2)


def _expert_kernel(be_ref, nv_ref, x_ref, wgu_ref, bgu_ref, wd_ref, bd_ref, y_ref, wgu_bf, wd_bf):
    i = pl.program_id(0)
    dff = wd_ref.shape[0]

    @pl.when(i < nv_ref[0])
    def _():
        changed = jnp.logical_or(i == 0, be_ref[i] != be_ref[jnp.maximum(i - 1, 0)])

        @pl.when(changed)
        def _():
            rows = 128

            def cast_gu(r, c):
                sl = pl.ds(pl.multiple_of(r * rows, rows), rows)
                wgu_bf[sl, :] = wgu_ref[sl, :].astype(BF16)
                return c

            def cast_d(r, c):
                sl = pl.ds(pl.multiple_of(r * rows, rows), rows)
                wd_bf[sl, :] = wd_ref[sl, :].astype(BF16)
                return c

            lax.fori_loop(0, wgu_ref.shape[0] // rows, cast_gu, 0)
            lax.fori_loop(0, wd_ref.shape[0] // rows, cast_d, 0)

        x = _load_row_tiles(x_ref, x_ref.shape[0] // SUBLANES).astype(BF16)
        g = jnp.minimum(_dot(x, wgu_bf[:, 0:dff]) + bgu_ref[:, 0:dff], SWIGLU_LIMIT)
        u = jnp.clip(_dot(x, wgu_bf[:, dff:2 * dff]) + bgu_ref[:, dff:2 * dff], -SWIGLU_LIMIT, SWIGLU_LIMIT)
        act = (u + 1.0) * (g * _sigmoid(SWIGLU_ALPHA * g))
        _store_row_tiles(y_ref, _dot(act.astype(BF16), wd_bf[...]) + bd_ref[...])

    @pl.when(i >= nv_ref[0])
    def _():
        y_ref[...] = jnp.zeros_like(y_ref)


def _experts(block_e, n_valid, xs, w_gu, b_gu, w_down, b_down):
    S = SUBLANES
    bm = EXPERT_BLOCK
    nb = xs.shape[0] // (bm * S)
    e, d, dff2 = w_gu.shape[1:]
    dff = w_down.shape[2]
    blk = lambda i, be, nv: (jnp.minimum(i, nv[0] - 1), 0)
    return pl.pallas_call(
        _expert_kernel,
        out_shape=jax.ShapeDtypeStruct(xs.shape, F32),
        grid_spec=pltpu.PrefetchScalarGridSpec(
            num_scalar_prefetch=2,
            grid=(nb,),
            in_specs=[pl.BlockSpec((bm * S, LANES), blk),
                      pl.BlockSpec((None, None, d, dff2), lambda i, be, nv: (0, be[i], 0, 0)),
                      pl.BlockSpec((None, 1, dff2), lambda i, be, nv: (be[i], 0, 0)),
                      pl.BlockSpec((None, None, dff, d), lambda i, be, nv: (0, be[i], 0, 0)),
                      pl.BlockSpec((None, 1, d), lambda i, be, nv: (be[i], 0, 0))],
            out_specs=pl.BlockSpec((bm * S, LANES), lambda i, be, nv: (i, 0)),
            scratch_shapes=[pltpu.VMEM((d, dff2), BF16), pltpu.VMEM((dff, d), BF16)]),
        compiler_params=_cparams(("arbitrary",)),
        name="experts",
    )(block_e, n_valid, xs, w_gu, b_gu.reshape(e, 1, dff2), w_down, b_down.reshape(e, 1, d))


def _combine_kernel(dest_hbm, ys_hbm, x1_ref, gate_ref, g2p_ref, g2s_ref, nf_ref,
                    yp_ref, ysm_ref, dest_smem, buf, sem_idx, sem_rows, *, n_prompt_tiles):
    R = x1_ref.shape[0]
    rec = dest_smem.shape[0] // 2
    i = pl.program_id(0)
    last = pl.num_programs(0) - 1
    slot = lax.rem(i, 2)

    def dest_load(tile, s):
        return pltpu.make_async_copy(dest_hbm.at[pl.ds(pl.multiple_of(tile * rec, rec), rec)],
                                     dest_smem.at[pl.ds(s * rec, rec)], sem_idx.at[s])

    S = SUBLANES

    def row_copy(s, k, t):
        return pltpu.make_async_copy(
            ys_hbm.at[pl.ds(pl.multiple_of(dest_smem[s * rec + k * R + t] * S, S), S)],
            buf.at[s, k, pl.ds(pl.multiple_of(t * S, S), S)], sem_rows.at[s])

    def gather(s):
        def issue(t, c):
            for k in range(TOP_K):
                row_copy(s, k, t).start(priority=k % 2)
            return c
        lax.fori_loop(0, R, issue, 0, unroll=DMA_UNROLL)

    @pl.when(i == 0)
    def _():
        dest_load(0, 0).start()
        dest_load(0, 0).wait()

        @pl.when(last > 0)
        def _():
            dest_load(1, 1).start()
        gather(0)

    def step(s):
        def drain(t, c):
            for k in range(TOP_K):
                row_copy(s, k, t).wait()
            return c

        lax.fori_loop(0, R, drain, 0, unroll=DMA_UNROLL)

        @pl.when(i < last)
        def _():
            dest_load(i + 1, 1 - s).wait()
            gather(1 - s)

            @pl.when(i + 2 <= last)
            def _():
                dest_load(i + 2, s).start()

        gate = gate_ref[...]
        ff = jnp.zeros(x1_ref.shape, F32)
        for k in range(TOP_K):
            ff = ff + _load_row_tiles(buf, R, lead=(s, k)) * gate[:, k:k + 1]
        is_prompt = i < n_prompt_tiles

        @pl.when(is_prompt)
        def _():
            yp_ref[...] = _rms(x1_ref[...] + g2p_ref[...] * ff, nf_ref[...])

        @pl.when(jnp.logical_not(is_prompt))
        def _():
            ysm_ref[...] = _rms(x1_ref[...] + g2s_ref[...] * ff, nf_ref[...])

    for s in range(2):
        pl.when(slot == s)(functools.partial(step, s))


def _combine(dest, ys, x1, gates, mod_p, mod_s, norm_final, n_prompt, rows_per_batch):
    n, d = x1.shape
    R = ROW_TILE
    dest, rec = _dest_records(dest, R)
    npt = n_prompt // R
    tiles = rows_per_batch // R
    nbatch = n_prompt // rows_per_batch
    n_s = n - n_prompt
    return pl.pallas_call(
        functools.partial(_combine_kernel, n_prompt_tiles=npt),
        out_shape=(jax.ShapeDtypeStruct((n_prompt, d), F32), jax.ShapeDtypeStruct((n_s, d), F32)),
        grid=(n // R,),
        in_specs=[pl.BlockSpec(memory_space=pl.ANY),
                  pl.BlockSpec(memory_space=pl.ANY),
                  pl.BlockSpec((R, d), lambda i: (i, 0)),
                  pl.BlockSpec((R, LANES), lambda i: (i, 0)),
                  pl.BlockSpec((None, 1, d), lambda i: (jnp.minimum(i // tiles, nbatch - 1), 0, 5)),
                  pl.BlockSpec((R, d), lambda i: (jnp.maximum(i - npt, 0), 5)),
                  pl.BlockSpec((1, d), lambda i: (0, 0))],
        out_specs=(pl.BlockSpec((R, d), lambda i: (jnp.minimum(i, npt - 1), 0)),
                   pl.BlockSpec((R, d), lambda i: (jnp.maximum(i - npt, 0), 0))),
        scratch_shapes=[pltpu.SMEM((2 * rec,), I32), pltpu.VMEM((2, TOP_K, R * SUBLANES, LANES), F32),
                        pltpu.SemaphoreType.DMA((2,)), pltpu.SemaphoreType.DMA((2,))],
        compiler_params=_cparams(("arbitrary",)),
        name="combine",
    )(dest, ys, x1, gates, mod_p, mod_s, norm_final)


def _reorder_w_in(w):
    W = GROUP_W
    g0 = 4 * W
    g1 = g0 + 2 * N_HEADS
    pad = jnp.zeros((w.shape[0], LANES - 2 * N_HEADS), w.dtype)
    cols = [w[:, 0:g0], w[:, g1:g1 + W], w[:, g1 + 2 * W:g1 + 4 * W], w[:, g1 + W:g1 + 2 * W], w[:, g0:g1], pad]
    return jnp.concatenate(cols, axis=1).astype(BF16)


def kernel(x_prompt, x_sample, c_prompt, c_sample, state_mlstm_C, state_mlstm_n, state_mlstm_m,
           state_hgrn_S, w_ada, b_ada, norm_mix, norm_ffn, w_in, b_gate, norm_a, lb_logits, norm_b,
           w_out, router_w, router_b, w_gu, b_gu, w_down, b_down, norm_final):
    bp, seq, d = x_prompt.shape
    bs = x_sample.shape[0]
    assert x_sample.shape[1] == 1 and w_ada.shape[0] == 1 and d == SUBLANES * LANES
    n_p = bp * seq
    n_all = n_p + bs
    n_exp = router_w.shape[2]
    W = GROUP_W
    n_a = 7 * W

    mod = _ada(jnp.concatenate([c_prompt, c_sample], axis=0), w_ada[0], b_ada)
    mod_p = mod[:bp].reshape(bp, 1, 6 * d)
    mod_s = mod[bp:]

    w_r = _reorder_w_in(w_in[0])
    nmix = norm_mix.reshape(1, d)
    tm = min(TOKEN_TILE, seq)
    xp = x_prompt.reshape(n_p, d)
    xs_ = x_sample.reshape(bs, d)
    pa_p, pb_p = _inproj(xp, mod_p, False, nmix, w_r, n_a, tm, seq, BF16)
    pa_s, pb_s = _inproj(xs_, mod_s, True, nmix, w_r, n_a, bs, None, F32)

    bg_row = jnp.pad(b_gate.reshape(1, 2 * N_HEADS), ((0, 0), (0, LANES - 2 * N_HEADS)))
    na = norm_a.reshape(1, W)
    nb_ = norm_b.reshape(1, W)
    ha_p, c_p, nrm_p, m_p = _mlstm_prompt(pa_p, pb_p, bg_row, na, bp, seq)
    hb_p, s_p = _hgrn_prompt(pa_p, pb_p, lb_logits, nb_, bp, seq)
    ha_s, hb_s, c_s, nrm_s, m_s, s_s = _step_mixers(
        pa_s, pb_s, bg_row, na, nb_, lb_logits, state_mlstm_C, state_mlstm_n, state_mlstm_m, state_hgrn_S)

    w_o = w_out[0].astype(BF16)
    rw = jnp.pad(router_w[0], ((0, 0), (0, LANES - n_exp))).astype(BF16)
    rb = jnp.pad(router_b.reshape(1, n_exp), ((0, 0), (0, LANES - n_exp)), constant_values=NEG)
    nffn = norm_ffn.reshape(1, d)
    x1, h2, idx, gates, rank, counts = _post(ha_p, hb_p, xp, mod_p, ha_s, hb_s, xs_, mod_s, nffn, w_o, rw, rb,
                                             tm, seq)

    bm = EXPERT_BLOCK
    cnt = counts[0].astype(I32)
    padded = ((cnt + bm - 1) // bm) * bm
    ends = jnp.cumsum(padded)
    start_row = (ends - padded).astype(F32).reshape(1, LANES)
    dest = _dest(idx, rank, start_row)
    n_blocks = (n_all * TOP_K + n_exp * (bm - 1) + bm - 1) // bm
    n_valid = (ends[n_exp - 1] // bm).astype(I32)
    blk_start = jnp.arange(n_blocks, dtype=I32) * bm
    block_e = jnp.sum((ends[None, :n_exp] <= blk_start[:, None]).astype(I32), axis=1)
    block_e = jnp.minimum(block_e, n_exp - 1)
    last_e = block_e[jnp.maximum(n_valid - 1, 0)]
    block_e = jnp.where(jnp.arange(n_blocks) < n_valid, block_e, last_e)

    min_valid = -(-(n_all * TOP_K) // bm)
    group_last = jnp.where(padded[:n_exp] > 0, ends[:n_exp] // bm - 1, -1)
    tail = n_valid + jnp.arange(n_blocks - min_valid, dtype=I32)
    tail = jnp.where(tail < n_blocks, tail, -1)
    zero_blocks = jnp.concatenate([group_last.astype(I32), tail])

    xs_sorted = _dispatch(zero_blocks, dest, h2, n_blocks * bm)
    ys_sorted = _experts(block_e, n_valid.reshape(1), xs_sorted, w_gu, b_gu[0], w_down, b_down[0])
    y_p, y_s = _combine(dest, ys_sorted, x1, gates, mod_p, mod_s, norm_final.reshape(1, d), n_p, seq)

    m_p_out = m_p[:, :N_HEADS, 0][None]
    return (y_p.reshape(bp, seq, d), y_s.reshape(bs, 1, d), c_p, nrm_p, m_p_out, s_p,
            c_s, nrm_s, m_s, s_s)
```

```python
import functools

import jax
import jax.numpy as jnp
from jax import lax
from jax.experimental import pallas as pl
from jax.experimental.pallas import tpu as pltpu

F32 = jnp.float32
BF16 = jnp.bfloat16
I32 = jnp.int32

EPS = 1e-6
NEG = -1e30
SWIGLU_LIMIT = 7.0
SWIGLU_ALPHA = 1.702
TOP_K = 4

LANES = 128
HEAD_DIM = 128
N_HEADS = 4
GROUP_W = N_HEADS * HEAD_DIM
VMEM_LIMIT = 56 * 1024 * 1024

MLSTM_CHUNK = 256
HGRN_BLOCK = 128
HGRN_SUB = 16
TOKEN_TILE = 512
ROW_TILE = 128
EXPERT_BLOCK = 512
STEP_BATCH = 8
DISPATCH_GROUP = 64
DMA_UNROLL = 8
DEST_ALIGN = 1024
DEST_TILE_MAX = 8192
STEP_ROWS = 32


def _cparams(sem, vmem=VMEM_LIMIT):
    return pltpu.CompilerParams(dimension_semantics=sem, vmem_limit_bytes=vmem)


def _dot(a, b):
    return jnp.dot(a, b, preferred_element_type=F32)


def _dot_nt(a, b):
    return lax.dot_general(a, b, (((1,), (1,)), ((), ())), preferred_element_type=F32)


def _dot_tn(a, b):
    return lax.dot_general(a, b, (((0,), (0,)), ((), ())), preferred_element_type=F32)


def _sigmoid(x):
    return 1.0 / (1.0 + jnp.exp(-x))


def _log_sigmoid(x):
    return jnp.minimum(x, 0.0) - jnp.log1p(jnp.exp(-jnp.abs(x)))


def _rms(x, g):
    return x * lax.rsqrt(jnp.mean(x * x, axis=-1, keepdims=True) + EPS) * g


def _cumsum_rows(tri, x):
    hi = x.astype(BF16)
    r1 = x - hi.astype(F32)
    mid = r1.astype(BF16)
    lo = (r1 - mid.astype(F32)).astype(BF16)
    return _dot(tri, hi) + _dot(tri, mid) + _dot(tri, lo)


SUBLANES = 8


def _store_row_tiles(ref, x):
    rows = x.shape[0]
    for c in range(SUBLANES):
        ref[pl.ds(c, rows, stride=SUBLANES), :] = x[:, c * LANES:(c + 1) * LANES]


def _load_row_tiles(ref, rows, lead=()):
    return jnp.concatenate([ref[lead + (pl.ds(c, rows, stride=SUBLANES), slice(None))]
                            for c in range(SUBLANES)], axis=-1)


def _ada_kernel(c_ref, w_ref, b_ref, o_ref):
    c = c_ref[...]
    a = (c * _sigmoid(c)).astype(BF16)
    o_ref[...] = _dot(a, w_ref[...].astype(BF16)) + b_ref[...]


def _ada(c_all, w, b):
    m, d = c_all.shape
    n = w.shape[1]
    tn = 1024
    return pl.pallas_call(
        _ada_kernel,
        out_shape=jax.ShapeDtypeStruct((m, n), F32),
        grid=(n // tn,),
        in_specs=[pl.BlockSpec((m, d), lambda j: (0, 0)),
                  pl.BlockSpec((d, tn), lambda j: (0, j)),
                  pl.BlockSpec((1, tn), lambda j: (0, j))],
        out_specs=pl.BlockSpec((m, tn), lambda j: (0, j)),
        compiler_params=_cparams(("parallel",)),
        name="ada",
    )(c_all, w, b)


def _inproj_kernel(x_ref, sh_ref, sc_ref, nw_ref, w_ref, oa_ref, ob_ref):
    h = _rms(x_ref[...], nw_ref[...]) * (1.0 + sc_ref[...]) + sh_ref[...]
    hb = h.astype(BF16)
    na = oa_ref.shape[1]
    nb = ob_ref.shape[1]
    for j in range(0, na, GROUP_W):
        oa_ref[:, j:j + GROUP_W] = _dot(hb, w_ref[:, j:j + GROUP_W]).astype(oa_ref.dtype)
    ob_ref[:, 0:GROUP_W] = _dot(hb, w_ref[:, na:na + GROUP_W])
    ob_ref[:, GROUP_W:nb] = _dot(hb, w_ref[:, na + GROUP_W:na + nb])


def _inproj(x, mod_rows, mod_is_per_row, norm_w, w_r, n_a, tm, rows_per_batch, out_dtype):
    n, d = x.shape
    n_b = w_r.shape[1] - n_a
    if mod_is_per_row:
        sh_spec = pl.BlockSpec((tm, d), lambda i: (i, 0))
        sc_spec = pl.BlockSpec((tm, d), lambda i: (i, 1))
    else:
        tiles = rows_per_batch // tm
        sh_spec = pl.BlockSpec((None, 1, d), lambda i: (i // tiles, 0, 0))
        sc_spec = pl.BlockSpec((None, 1, d), lambda i: (i // tiles, 0, 1))
    return pl.pallas_call(
        _inproj_kernel,
        out_shape=(jax.ShapeDtypeStruct((n, n_a), out_dtype),
                   jax.ShapeDtypeStruct((n, n_b), F32)),
        grid=(n // tm,),
        in_specs=[pl.BlockSpec((tm, d), lambda i: (i, 0)), sh_spec, sc_spec,
                  pl.BlockSpec((1, d), lambda i: (0, 0)),
                  pl.BlockSpec(w_r.shape, lambda i: (0, 0))],
        out_specs=(pl.BlockSpec((tm, n_a), lambda i: (i, 0)),
                   pl.BlockSpec((tm, n_b), lambda i: (i, 0))),
        compiler_params=_cparams(("parallel",)),
        name="inproj",
    )(x, mod_rows, mod_rows, norm_w, w_r)


def _mlstm_kernel(q_ref, k_ref, v_ref, o_ref, g_ref, bg_ref, na_ref,
                  h_ref, c_ref, n_ref, m_ref):
    L = q_ref.shape[0]
    scale = HEAD_DIM ** -0.5

    @pl.when(pl.program_id(1) == 0)
    def _():
        c_ref[...] = jnp.zeros_like(c_ref)
        n_ref[...] = jnp.zeros_like(n_ref)
        m_ref[...] = jnp.full_like(m_ref, NEG)

    g = g_ref[...] + bg_ref[...]
    lane = lax.broadcasted_iota(I32, (L, LANES), 1)
    gates = jnp.where(lane < N_HEADS, g, _log_sigmoid(g))
    row = lax.broadcasted_iota(I32, (L, L), 0)
    col = lax.broadcasted_iota(I32, (L, L), 1)
    causal = row >= col
    tri = causal.astype(BF16)
    csum = _cumsum_rows(tri, gates)
    gates_t = gates.T
    csum_t = csum.T

    for h in range(N_HEADS):
        sl = slice(h * HEAD_DIM, (h + 1) * HEAD_DIM)
        qh, kh, vh = q_ref[:, sl], k_ref[:, sl], v_ref[:, sl]
        b_col = csum[:, N_HEADS + h:N_HEADS + h + 1]
        li_col = gates[:, h:h + 1]
        b_row = csum_t[N_HEADS + h:N_HEADS + h + 1, :]
        li_row = gates_t[h:h + 1, :]
        m_prev = m_ref[h:h + 1, 0:1]
        c_prev = c_ref[h]
        n_prev = n_ref[h:h + 1, :]

        dm = jnp.where(causal, b_col - b_row + li_row, NEG)
        inter = b_col + m_prev
        m_t = jnp.maximum(inter, jnp.max(dm, axis=-1, keepdims=True))
        w = jnp.exp(dm - m_t) * (_dot_nt(qh, kh) * scale)
        wi = jnp.exp(inter - m_t)
        num = _dot(w.astype(BF16), vh) + wi * _dot(qh, c_prev.astype(BF16))
        den = (jnp.sum(w, axis=-1, keepdims=True)
               + wi * jnp.sum(qh.astype(F32) * n_prev, axis=-1, keepdims=True))
        hval = num / jnp.maximum(jnp.abs(den), jnp.exp(-m_t))

        m_new = m_t[L - 1:L, :]
        b_last = b_col[L - 1:L, :]
        ws = jnp.exp(b_last - b_col + li_col - m_new)
        dec = jnp.exp(b_last + m_prev - m_new)
        kw = kh.astype(F32) * (ws * scale)
        c_ref[h] = dec * c_prev + _dot_tn(kw.astype(BF16), vh)
        n_ref[h:h + 1, :] = dec * n_prev + jnp.sum(kw, axis=0, keepdims=True)
        m_ref[h:h + 1, :] = jnp.broadcast_to(m_new, (1, LANES))

        hn = _rms(hval, na_ref[:, sl]) * _sigmoid(o_ref[:, sl].astype(F32))
        h_ref[:, sl] = hn.astype(h_ref.dtype)


def _mlstm_prompt(pa, pb, b_gate_row, norm_a, bsz, seq):
    L = MLSTM_CHUNK
    nc = seq // L
    n = bsz * seq
    gate_blk = (pb.shape[1] - LANES) // LANES

    def col(j):
        return pl.BlockSpec((L, GROUP_W), lambda b, c: (b * nc + c, j))

    return pl.pallas_call(
        _mlstm_kernel,
        out_shape=(jax.ShapeDtypeStruct((n, GROUP_W), BF16),
                   jax.ShapeDtypeStruct((1, bsz, N_HEADS, HEAD_DIM, HEAD_DIM), F32),
                   jax.ShapeDtypeStruct((1, bsz, N_HEADS, HEAD_DIM), F32),
                   jax.ShapeDtypeStruct((bsz, 8, LANES), F32)),
        grid=(bsz, nc),
        in_specs=[col(0), col(1), col(2), col(3),
                  pl.BlockSpec((L, LANES), lambda b, c: (b * nc + c, gate_blk)),
                  pl.BlockSpec((1, LANES), lambda b, c: (0, 0)),
                  pl.BlockSpec((1, GROUP_W), lambda b, c: (0, 0))],
        out_specs=(pl.BlockSpec((L, GROUP_W), lambda b, c: (b * nc + c, 0)),
                   pl.BlockSpec((None, None, N_HEADS, HEAD_DIM, HEAD_DIM), lambda b, c: (0, b, 0, 0, 0)),
                   pl.BlockSpec((None, None, N_HEADS, HEAD_DIM), lambda b, c: (0, b, 0, 0)),
                   pl.BlockSpec((None, 8, LANES), lambda b, c: (b, 0, 0))),
        compiler_params=_cparams(("parallel", "arbitrary")),
        name="mlstm_prompt",
    )(pa, pa, pa, pa, pb, b_gate_row, norm_a)


def _lower_bound(lb_logits_ref):
    lg = lb_logits_ref[...]
    e = jnp.exp(lg - jnp.max(lg, axis=0, keepdims=True))
    return e[0:1, :] / jnp.sum(e, axis=0, keepdims=True)


def _hgrn_kernel(q_ref, v_ref, g_ref, f_ref, lbl_ref, nb_ref, h_ref, s_ref, st_ref):
    LB = q_ref.shape[0]
    C = HGRN_SUB
    NS = LB // C
    H2 = C // 2
    assert LB == LANES
    scale = HEAD_DIM ** -0.5

    @pl.when(pl.program_id(1) == 0)
    def _():
        st_ref[...] = jnp.zeros_like(st_ref)

    lb = _lower_bound(lbl_ref)
    f = lb + (1.0 - lb) * _sigmoid(f_ref[...])
    qraw = q_ref[...].astype(F32)
    q_all = qraw * _sigmoid(qraw) * scale
    k_all = 1.0 - f
    row = lax.broadcasted_iota(I32, (LB, LB), 0)
    col = lax.broadcasted_iota(I32, (LB, LB), 1)
    b_all = _cumsum_rows((row >= col).astype(BF16), jnp.log(f))

    rl = lax.broadcasted_iota(I32, (C, LANES), 0)
    cl = lax.broadcasted_iota(I32, (C, LANES), 1)
    ones_bf = jnp.ones((HEAD_DIM, LANES), BF16)

    for h in range(N_HEADS):
        sl = slice(h * HEAD_DIM, (h + 1) * HEAD_DIM)
        qh, kh, bh = q_all[:, sl], k_all[:, sl], b_all[:, sl]
        vh = v_ref[:, sl]
        st = st_ref[h]

        parts = []
        for i in range(NS):
            q_i, k_i, b_i = (x[i * C:(i + 1) * C, :] for x in (qh, kh, bh))
            for s in range(C):
                lo = 0 if s < H2 else H2
                parts.append(q_i[lo:, :] * k_i[s:s + 1, :]
                             * jnp.exp(jnp.minimum(b_i[lo:, :] - b_i[s:s + 1, :], 0.0)))
        sums = _dot(jnp.concatenate(parts, axis=0).astype(BF16), ones_bf)

        a_rows = []
        off = 0
        for i in range(NS):
            a = jnp.zeros((C, LANES), F32)
            for s in range(C):
                lo = 0 if s < H2 else H2
                blk = sums[off:off + C - lo, :]
                if lo:
                    blk = jnp.concatenate([jnp.zeros((lo, LANES), F32), blk], axis=0)
                a = jnp.where(cl == i * C + s, blk, a)
                off += C - lo
            a = jnp.where(rl + i * C >= cl, a, 0.0)
            if i > 0:
                b_i = bh[i * C:(i + 1) * C, :]
                r_i = b_i[0:1, :]
                qs = qh[i * C:(i + 1) * C, :] * jnp.exp(b_i - r_i)
                ks = kh[0:i * C, :] * jnp.exp(jnp.minimum(r_i - bh[0:i * C, :], 0.0))
                ks = jnp.concatenate([ks, jnp.zeros((LB - i * C, HEAD_DIM), F32)], axis=0)
                a = a + _dot_nt(qs.astype(BF16), ks.astype(BF16))
            a_rows.append(a)
        a_full = jnp.concatenate(a_rows, axis=0)

        o = _dot(a_full.astype(BF16), vh) + _dot_nt((qh * jnp.exp(bh)).astype(BF16), st.astype(BF16))
        b_l = bh[LB - 1:LB, :]
        kd = kh * jnp.exp(b_l - bh)
        st_ref[h] = st * jnp.exp(b_l) + _dot_tn(vh, kd.astype(BF16))

        gv = g_ref[:, sl].astype(F32)
        hn = _rms(o, nb_ref[:, sl]) * (gv * _sigmoid(gv))
        h_ref[:, sl] = hn.astype(h_ref.dtype)

    @pl.when(pl.program_id(1) == pl.num_programs(1) - 1)
    def _():
        for h in range(N_HEADS):
            s_ref[h] = st_ref[h].T


def _hgrn_prompt(pa, pb, lb_logits, norm_b, bsz, seq):
    LB = HGRN_BLOCK
    nc = seq // LB
    n = bsz * seq

    def col(j):
        return pl.BlockSpec((LB, GROUP_W), lambda b, c: (b * nc + c, j))

    return pl.pallas_call(
        _hgrn_kernel,
        out_shape=(jax.ShapeDtypeStruct((n, GROUP_W), BF16),
                   jax.ShapeDtypeStruct((1, bsz, N_HEADS, HEAD_DIM, HEAD_DIM), F32)),
        grid=(bsz, nc),
        in_specs=[col(4), col(5), col(6),
                  pl.BlockSpec((LB, GROUP_W), lambda b, c: (b * nc + c, 0)),
                  pl.BlockSpec(lb_logits.shape, lambda b, c: (0, 0)),
                  pl.BlockSpec((1, GROUP_W), lambda b, c: (0, 0))],
        out_specs=(pl.BlockSpec((LB, GROUP_W), lambda b, c: (b * nc + c, 0)),
                   pl.BlockSpec((None, None, N_HEADS, HEAD_DIM, HEAD_DIM), lambda b, c: (0, b, 0, 0, 0))),
        scratch_shapes=[pltpu.VMEM((N_HEADS, HEAD_DIM, HEAD_DIM), F32)],
        compiler_params=_cparams(("parallel", "arbitrary")),
        name="hgrn_prompt",
    )(pa, pa, pa, pb, lb_logits, norm_b)


def _step_kernel(pa_ref, pb_ref, bg_ref, na_ref, nb_ref, lbl_ref,
                 c0_ref, n0_ref, m0_ref, s0_ref,
                 ha_ref, hb_ref, c1_ref, n1_ref, m1_ref, s1_ref, a_ref, b_ref, q_ref):
    scale = HEAD_DIM ** -0.5
    W = GROUP_W
    H = N_HEADS
    D = HEAD_DIM
    CB = 3 * D
    lb = _lower_bound(lbl_ref)
    gates_all = pb_ref[:, W:W + LANES] + bg_ref[...]
    f_all = lb + (1.0 - lb) * _sigmoid(pb_ref[:, 0:W])
    a_ref[...] = jnp.zeros_like(a_ref)
    b_ref[...] = jnp.zeros_like(b_ref)
    q_ref[...] = jnp.zeros_like(q_ref)
    for j in range(STEP_BATCH):
        for h in range(H):
            for part in range(3):
                b_ref[j, (2 + part) * H + h:(2 + part) * H + h + 1, h * CB + 2 * D:h * CB + 3 * D] = (
                    jnp.ones((1, D), F32))

    for j in range(STEP_BATCH):
        row = slice(j, j + 1)
        aux = []
        for h in range(H):
            q = pa_ref[row, h * D:(h + 1) * D]
            k = pa_ref[row, W + h * D:W + (h + 1) * D]
            v = pa_ref[row, 2 * W + h * D:2 * W + (h + 1) * D]
            li = gates_all[row, h:h + 1]
            lf = _log_sigmoid(gates_all[row, H + h:H + h + 1])
            inter = lf + m0_ref[row, h:h + 1]
            m_t = jnp.maximum(inter, li)
            ws = jnp.exp(li - m_t)
            dec = jnp.exp(inter - m_t)
            kw = k * (ws * scale)

            qraw = pa_ref[row, 4 * W + h * D:4 * W + (h + 1) * D]
            qb = qraw * _sigmoid(qraw) * scale
            vb = pa_ref[row, 5 * W + h * D:5 * W + (h + 1) * D]
            f = f_all[row, h * D:(h + 1) * D]
            decay = jnp.exp(jnp.log(f))
            kb = 1.0 - f
            d_hi = decay.astype(BF16).astype(F32)
            d_mid = (decay - d_hi).astype(BF16).astype(F32)

            a_ref[j, h:h + 1, :] = kw
            a_ref[j, H + h:H + h + 1, :] = kb
            a_ref[j, 2 * H + h:2 * H + h + 1, :] = d_hi
            a_ref[j, 3 * H + h:3 * H + h + 1, :] = d_mid
            a_ref[j, 4 * H + h:4 * H + h + 1, :] = decay - d_hi - d_mid
            b_ref[j, h:h + 1, h * CB:h * CB + D] = v
            b_ref[j, H + h:H + h + 1, h * CB + D:h * CB + 2 * D] = vb
            q_ref[j, h:h + 1, :] = q
            q_ref[j, H + h:H + h + 1, :] = qb * decay
            aux.append((q, k, v, kw, m_t, ws, dec, qb, kb, vb))

        upd = _dot_tn(a_ref[j].astype(BF16), b_ref[j].astype(BF16))
        q_rows = q_ref[j].astype(BF16)
        for h in range(H):
            sl = slice(h * D, (h + 1) * D)
            q, k, v, kw, m_t, ws, dec, qb, kb, vb = aux[h]
            og = pa_ref[row, 3 * W + h * D:3 * W + (h + 1) * D]
            c0 = c0_ref[j, h]
            n0 = n0_ref[j, h:h + 1, :]
            w = ws * (jnp.sum(q * k, axis=-1, keepdims=True) * scale)
            num = w * v + dec * _dot(q_rows, c0.astype(BF16))[h:h + 1, :]
            den = w + dec * jnp.sum(q * n0, axis=-1, keepdims=True)
            hval = num / jnp.maximum(jnp.abs(den), jnp.exp(-m_t))
            c1_ref[j, h] = dec * c0 + upd[:, h * CB:h * CB + D]
            n1_ref[j, h:h + 1, :] = dec * n0 + kw
            m1_ref[row, h:h + 1] = m_t
            ha_ref[row, sl] = _rms(hval, na_ref[:, sl]) * _sigmoid(og)
            gv = pa_ref[row, 6 * W + h * D:6 * W + (h + 1) * D]
            s0 = s0_ref[j, h]
            a = jnp.sum(qb * kb, axis=-1, keepdims=True)
            o = a * vb + _dot(q_rows, s0.astype(BF16))[H + h:H + h + 1, :]
            s1_ref[j, h] = upd[:, h * CB + 2 * D:h * CB + 3 * D] * s0 + upd[:, h * CB + D:h * CB + 2 * D]
            hb_ref[row, sl] = _rms(o, nb_ref[:, sl]) * (gv * _sigmoid(gv))


def _step_mixers(pa, pb, b_gate_row, norm_a, norm_b, lb_logits, c0, n0, m0, s0):
    bs = pa.shape[0]
    sb = STEP_BATCH
    st5 = pl.BlockSpec((None, sb, N_HEADS, HEAD_DIM, HEAD_DIM), lambda i: (0, i, 0, 0, 0))
    st4 = pl.BlockSpec((None, sb, N_HEADS, HEAD_DIM), lambda i: (0, i, 0, 0))
    st3 = pl.BlockSpec((None, sb, N_HEADS), lambda i: (0, i, 0))
    rowblk = lambda w: pl.BlockSpec((sb, w), lambda i: (i, 0))
    const = lambda a: pl.BlockSpec(a.shape, lambda i: (0,) * a.ndim)
    return pl.pallas_call(
        _step_kernel,
        out_shape=(jax.ShapeDtypeStruct((bs, GROUP_W), F32),
                   jax.ShapeDtypeStruct((bs, GROUP_W), F32),
                   jax.ShapeDtypeStruct(c0.shape, F32),
                   jax.ShapeDtypeStruct(n0.shape, F32),
                   jax.ShapeDtypeStruct(m0.shape, F32),
                   jax.ShapeDtypeStruct(s0.shape, F32)),
        grid=(bs // sb,),
        in_specs=[rowblk(pa.shape[1]), rowblk(pb.shape[1]), const(b_gate_row), const(norm_a),
                  const(norm_b), const(lb_logits), st5, st4, st3, st5],
        out_specs=(rowblk(GROUP_W), rowblk(GROUP_W), st5, st4, st3, st5),
        scratch_shapes=[pltpu.VMEM((sb, STEP_ROWS, HEAD_DIM), F32),
                        pltpu.VMEM((sb, STEP_ROWS, 3 * N_HEADS * HEAD_DIM), F32),
                        pltpu.VMEM((sb, 16, HEAD_DIM), F32)],
        compiler_params=_cparams(("parallel",)),
        name="step_mixers",
    )(pa, pb, b_gate_row, norm_a, norm_b, lb_logits, c0, n0, m0, s0)


def _post_rows(ha_ref, hb_ref, x_ref, g1_ref, sh2_ref, sc2_ref, nf_ref, wo_ref, rw_ref, rb_ref,
               x1_ref, h2_ref, idx_ref, gate_ref, rank_ref, cnt_ref, base_ref):
    tm = x_ref.shape[0]
    mix = (_dot(ha_ref[...].astype(BF16), wo_ref[0:GROUP_W, :])
           + _dot(hb_ref[...].astype(BF16), wo_ref[GROUP_W:2 * GROUP_W, :]))
    x1 = x_ref[...] + g1_ref[...] * mix
    x1_ref[0:tm, :] = x1
    h2 = _rms(x1, nf_ref[...]) * (1.0 + sc2_ref[...]) + sh2_ref[...]
    _store_row_tiles(h2_ref, h2)
    logits = _dot(h2.astype(BF16), rw_ref[...]) + rb_ref[...]
    lane = lax.broadcasted_iota(I32, (tm, LANES), 1)
    lane_f = lane.astype(F32)
    cur = logits
    vals, ids = [], []
    for _ in range(TOP_K):
        mx = jnp.max(cur, axis=-1, keepdims=True)
        am = jnp.min(jnp.where(cur == mx, lane_f, float(LANES)), axis=-1, keepdims=True)
        vals.append(mx)
        ids.append(am)
        cur = jnp.where(lane_f == am, -jnp.inf, cur)
    es = [jnp.exp(v - vals[0]) for v in vals]
    tot = es[0] + es[1] + es[2] + es[3]
    idx_out = jnp.full((tm, LANES), -1.0, F32)
    gate_out = jnp.zeros((tm, LANES), F32)
    for k in range(TOP_K):
        idx_out = jnp.where(lane == k, ids[k], idx_out)
        gate_out = jnp.where(lane == k, es[k] / tot, gate_out)
    idx_ref[0:tm, :] = idx_out.astype(I32)
    gate_ref[0:tm, :] = gate_out

    hits = [lane_f == ids[k] for k in range(TOP_K)]
    onehot = jnp.zeros((tm, LANES), F32)
    for hk in hits:
        onehot = onehot + hk.astype(F32)
    row = lax.broadcasted_iota(I32, (tm, tm), 0)
    col = lax.broadcasted_iota(I32, (tm, tm), 1)
    before = _dot((row > col).astype(BF16), onehot.astype(BF16)) + base_ref[...]
    rank = jnp.zeros((tm, LANES), F32)
    for k, hk in enumerate(hits):
        rank = jnp.where(lane == k, jnp.sum(jnp.where(hk, before, 0.0), axis=-1, keepdims=True), rank)
    rank_ref[0:tm, :] = rank
    base = base_ref[...] + jnp.sum(onehot, axis=0, keepdims=True)
    base_ref[...] = base
    cnt_ref[...] = base


def _post_kernel(ha_ref, hb_ref, x_ref, g1_ref, sh2_ref, sc2_ref,
                 has_ref, hbs_ref, xs_ref, g1s_ref, sh2s_ref, sc2s_ref,
                 nf_ref, wo_ref, rw_ref, rb_ref, x1_ref, h2_ref, idx_ref, gate_ref, rank_ref, cnt_ref,
                 base_ref, *, n_prompt_tiles):
    i = pl.program_id(0)
    shared = (nf_ref, wo_ref, rw_ref, rb_ref, x1_ref, h2_ref, idx_ref, gate_ref, rank_ref, cnt_ref, base_ref)

    @pl.when(i == 0)
    def _():
        base_ref[...] = jnp.zeros_like(base_ref)

    @pl.when(i < n_prompt_tiles)
    def _():
        _post_rows(ha_ref, hb_ref, x_ref, g1_ref, sh2_ref, sc2_ref, *shared)

    @pl.when(i == n_prompt_tiles)
    def _():
        _post_rows(has_ref, hbs_ref, xs_ref, g1s_ref, sh2s_ref, sc2s_ref, *shared)


def _post(ha_p, hb_p, x_p, mod_p, ha_s, hb_s, x_s, mod_s, norm_ffn, w_out, rw, rb, tm, rows_per_batch):
    n_p, d = x_p.shape
    n_s = x_s.shape[0]
    assert n_s <= tm
    npt = n_p // tm
    tiles = rows_per_batch // tm
    nbatch = n_p // rows_per_batch
    n_all = n_p + n_s
    prow = lambda w: pl.BlockSpec((tm, w), lambda i: (jnp.minimum(i, npt - 1), 0))
    pmod = lambda j: pl.BlockSpec((None, 1, d), lambda i: (jnp.minimum(i // tiles, nbatch - 1), 0, j))
    smod = lambda j: pl.BlockSpec((n_s, d), lambda i: (0, j))
    const = lambda a: pl.BlockSpec(a.shape, lambda i: (0,) * a.ndim)
    out_blk = lambda w: pl.BlockSpec((tm, w), lambda i: (i, 0))
    return pl.pallas_call(
        functools.partial(_post_kernel, n_prompt_tiles=npt),
        out_shape=(jax.ShapeDtypeStruct((n_all, d), F32),
                   jax.ShapeDtypeStruct((n_all * SUBLANES, LANES), F32),
                   jax.ShapeDtypeStruct((n_all, LANES), I32),
                   jax.ShapeDtypeStruct((n_all, LANES), F32),
                   jax.ShapeDtypeStruct((n_all, LANES), F32),
                   jax.ShapeDtypeStruct((1, LANES), F32)),
        grid=(npt + 1,),
        in_specs=[prow(GROUP_W), prow(GROUP_W), prow(d), pmod(2), pmod(3), pmod(4),
                  const(ha_s), const(hb_s), const(x_s), smod(2), smod(3), smod(4),
                  const(norm_ffn), const(w_out), const(rw), const(rb)],
        out_specs=(out_blk(d), pl.BlockSpec((tm * SUBLANES, LANES), lambda i: (i, 0)),
                   out_blk(LANES), out_blk(LANES), out_blk(LANES),
                   pl.BlockSpec((1, LANES), lambda i: (0, 0))),
        scratch_shapes=[pltpu.VMEM((1, LANES), F32)],
        compiler_params=_cparams(("arbitrary",)),
        name="post",
    )(ha_p, hb_p, x_p, mod_p, mod_p, mod_p, ha_s, hb_s, x_s, mod_s, mod_s, mod_s, norm_ffn, w_out, rw, rb)


def _route_tile(n):
    return max(r for r in range(LANES, 4 * LANES + 1, LANES) if n % r == 0)


def _dest_kernel(idx_ref, rank_ref, start_ref, dest_ref):
    R = idx_ref.shape[0]
    idx = idx_ref[...]
    lane = lax.broadcasted_iota(I32, (R, LANES), 1)
    dest = rank_ref[...]
    for k in range(TOP_K):
        st = jnp.sum(jnp.where(lane == idx[:, k:k + 1], start_ref[...], 0.0), axis=-1, keepdims=True)
        dest = dest + jnp.where(lane == k, st, 0.0)
    dest_ref[...] = dest.T[0:8, :].astype(I32)


def _dest(idx, rank, start_row):
    n = idx.shape[0]
    R = max(r for r in range(LANES, DEST_TILE_MAX + 1, LANES) if n % r == 0)
    return pl.pallas_call(
        _dest_kernel,
        out_shape=jax.ShapeDtypeStruct((8, n), I32),
        grid=(n // R,),
        in_specs=[pl.BlockSpec((R, LANES), lambda i: (i, 0)),
                  pl.BlockSpec((R, LANES), lambda i: (i, 0)),
                  pl.BlockSpec((1, LANES), lambda i: (0, 0))],
        out_specs=pl.BlockSpec((8, R), lambda i: (0, i)),
        compiler_params=_cparams(("parallel",)),
        name="dest",
    )(idx, rank, start_row)


def _dispatch_kernel(zb_ref, dest_hbm, x_ref, xs_hbm, dest_smem, zeros, sem_idx, sem_rows, sem_zero):
    S = SUBLANES
    R = x_ref.shape[0] // S
    G = DISPATCH_GROUP
    rec = dest_smem.shape[0] // 2
    bm = zeros.shape[0]
    i = pl.program_id(0)
    last = pl.num_programs(0) - 1
    slot = lax.rem(i, 2)

    def dest_load(tile, s):
        return pltpu.make_async_copy(dest_hbm.at[pl.ds(pl.multiple_of(tile * rec, rec), rec)],
                                     dest_smem.at[pl.ds(s * rec, rec)], sem_idx.at[s])

    @pl.when(i == 0)
    def _():
        zeros[...] = jnp.zeros_like(zeros)

        def zero_copy(j):
            return pltpu.make_async_copy(zeros, xs_hbm.at[pl.ds(pl.multiple_of(zb_ref[j] * bm, bm), bm)], sem_zero)

        def start(j, c):
            @pl.when(zb_ref[j] >= 0)
            def _():
                zero_copy(j).start()
            return c

        def wait(j, c):
            @pl.when(zb_ref[j] >= 0)
            def _():
                zero_copy(j).wait()
            return c

        lax.fori_loop(0, zb_ref.shape[0], start, 0)
        lax.fori_loop(0, zb_ref.shape[0], wait, 0)
        dest_load(0, 0).start()

    def step(s):
        dest_load(i, s).wait()

        @pl.when(i < last)
        def _():
            dest_load(i + 1, 1 - s).start()

        def row_copies(t, g):
            src = x_ref.at[pl.ds(pl.multiple_of(t * S, S), S)]
            return [pltpu.make_async_copy(
                        src, xs_hbm.at[pl.ds(pl.multiple_of(dest_smem[s * rec + k * R + t] * S, S), S)],
                        sem_rows.at[g % 2])
                    for k in range(TOP_K)]

        def issue(g):
            def body(t, c):
                for k, cp in enumerate(row_copies(t, g)):
                    cp.start(priority=k % 2)
                return c
            lax.fori_loop(g * G, (g + 1) * G, body, 0, unroll=DMA_UNROLL)

        def drain(g):
            def body(t, c):
                for cp in row_copies(t, g):
                    cp.wait()
                return c
            lax.fori_loop(g * G, (g + 1) * G, body, 0, unroll=DMA_UNROLL)

        n_groups = R // G
        for g in range(n_groups):
            issue(g)
            if g > 0:
                drain(g - 1)
        drain(n_groups - 1)

    for s in range(2):
        pl.when(slot == s)(functools.partial(step, s))


def _dest_records(dest, R):
    n = dest.shape[1]
    rec = -(-TOP_K * R // DEST_ALIGN) * DEST_ALIGN
    d = dest[:TOP_K].reshape(TOP_K, n // R, R).transpose(1, 0, 2).reshape(n // R, TOP_K * R)
    return jnp.pad(d, ((0, 0), (0, rec - TOP_K * R))).reshape(-1), rec


def _dispatch(zero_blocks, dest, h2, p_rows):
    S = SUBLANES
    n = h2.shape[0] // S
    R = _route_tile(n)
    assert R % DISPATCH_GROUP == 0
    dest, rec = _dest_records(dest, R)
    return pl.pallas_call(
        _dispatch_kernel,
        out_shape=jax.ShapeDtypeStruct((p_rows * S, LANES), F32),
        grid_spec=pltpu.PrefetchScalarGridSpec(
            num_scalar_prefetch=1,
            grid=(n // R,),
            in_specs=[pl.BlockSpec(memory_space=pl.ANY),
                      pl.BlockSpec((R * S, LANES), lambda i, zb: (i, 0))],
            out_specs=pl.BlockSpec(memory_space=pl.ANY),
            scratch_shapes=[pltpu.SMEM((2 * rec,), I32), pltpu.VMEM((EXPERT_BLOCK * S, LANES), F32),
                            pltpu.SemaphoreType.DMA((2,)), pltpu.SemaphoreType.DMA((2,)), pltpu.SemaphoreType.DMA]),
        compiler_params=_cparams(("arbitrary",)),
        name="dispatch",
    )(zero_blocks, dest, h2)


def _expert_kernel(be_ref, nv_ref, x_ref, wgu_ref, bgu_ref, wd_ref, bd_ref, y_ref, wgu_bf, wd_bf):
    i = pl.program_id(0)
    dff = wd_ref.shape[0]

    @pl.when(i < nv_ref[0])
    def _():
        changed = jnp.logical_or(i == 0, be_ref[i] != be_ref[jnp.maximum(i - 1, 0)])

        @pl.when(changed)
        def _():
            rows = 128

            def cast_gu(r, c):
                sl = pl.ds(pl.multiple_of(r * rows, rows), rows)
                wgu_bf[sl, :] = wgu_ref[sl, :].astype(BF16)
                return c

            def cast_d(r, c):
                sl = pl.ds(pl.multiple_of(r * rows, rows), rows)
                wd_bf[sl, :] = wd_ref[sl, :].astype(BF16)
                return c

            lax.fori_loop(0, wgu_ref.shape[0] // rows, cast_gu, 0)
            lax.fori_loop(0, wd_ref.shape[0] // rows, cast_d, 0)

        x = _load_row_tiles(x_ref, x_ref.shape[0] // SUBLANES).astype(BF16)
        g = jnp.minimum(_dot(x, wgu_bf[:, 0:dff]) + bgu_ref[:, 0:dff], SWIGLU_LIMIT)
        u = jnp.clip(_dot(x, wgu_bf[:, dff:2 * dff]) + bgu_ref[:, dff:2 * dff], -SWIGLU_LIMIT, SWIGLU_LIMIT)
        act = (u + 1.0) * (g * _sigmoid(SWIGLU_ALPHA * g))
        _store_row_tiles(y_ref, _dot(act.astype(BF16), wd_bf[...]) + bd_ref[...])

    @pl.when(i >= nv_ref[0])
    def _():
        y_ref[...] = jnp.zeros_like(y_ref)


def _experts(block_e, n_valid, xs, w_gu, b_gu, w_down, b_down):
    S = SUBLANES
    bm = EXPERT_BLOCK
    nb = xs.shape[0] // (bm * S)
    e, d, dff2 = w_gu.shape[1:]
    dff = w_down.shape[2]
    blk = lambda i, be, nv: (jnp.minimum(i, nv[0] - 1), 0)
    return pl.pallas_call(
        _expert_kernel,
        out_shape=jax.ShapeDtypeStruct(xs.shape, F32),
        grid_spec=pltpu.PrefetchScalarGridSpec(
            num_scalar_prefetch=2,
            grid=(nb,),
            in_specs=[pl.BlockSpec((bm * S, LANES), blk),
                      pl.BlockSpec((None, None, d, dff2), lambda i, be, nv: (0, be[i], 0, 0)),
                      pl.BlockSpec((None, 1, dff2), lambda i, be, nv: (be[i], 0, 0)),
                      pl.BlockSpec((None, None, dff, d), lambda i, be, nv: (0, be[i], 0, 0)),
                      pl.BlockSpec((None, 1, d), lambda i, be, nv: (be[i], 0, 0))],
            out_specs=pl.BlockSpec((bm * S, LANES), lambda i, be, nv: (i, 0)),
            scratch_shapes=[pltpu.VMEM((d, dff2), BF16), pltpu.VMEM((dff, d), BF16)]),
        compiler_params=_cparams(("arbitrary",)),
        name="experts",
    )(block_e, n_valid, xs, w_gu, b_gu.reshape(e, 1, dff2), w_down, b_down.reshape(e, 1, d))


def _combine_kernel(dest_hbm, ys_hbm, x1_ref, gate_ref, g2p_ref, g2s_ref, nf_ref,
                    yp_ref, ysm_ref, dest_smem, buf, sem_idx, sem_rows, *, n_prompt_tiles):
    R = x1_ref.shape[0]
    rec = dest_smem.shape[0] // 2
    i = pl.program_id(0)
    last = pl.num_programs(0) - 1
    slot = lax.rem(i, 2)

    def dest_load(tile, s):
        return pltpu.make_async_copy(dest_hbm.at[pl.ds(pl.multiple_of(tile * rec, rec), rec)],
                                     dest_smem.at[pl.ds(s * rec, rec)], sem_idx.at[s])

    S = SUBLANES

    def row_copy(s, k, t):
        return pltpu.make_async_copy(
            ys_hbm.at[pl.ds(pl.multiple_of(dest_smem[s * rec + k * R + t] * S, S), S)],
            buf.at[s, k, pl.ds(pl.multiple_of(t * S, S), S)], sem_rows.at[s])

    def gather(s):
        def issue(t, c):
            for k in range(TOP_K):
                row_copy(s, k, t).start(priority=k % 2)
            return c
        lax.fori_loop(0, R, issue, 0, unroll=DMA_UNROLL)

    @pl.when(i == 0)
    def _():
        dest_load(0, 0).start()
        dest_load(0, 0).wait()

        @pl.when(last > 0)
        def _():
            dest_load(1, 1).start()
        gather(0)

    def step(s):
        def drain(t, c):
            for k in range(TOP_K):
                row_copy(s, k, t).wait()
            return c

        lax.fori_loop(0, R, drain, 0, unroll=DMA_UNROLL)

        @pl.when(i < last)
        def _():
            dest_load(i + 1, 1 - s).wait()
            gather(1 - s)

            @pl.when(i + 2 <= last)
            def _():
                dest_load(i + 2, s).start()

        gate = gate_ref[...]
        ff = jnp.zeros(x1_ref.shape, F32)
        for k in range(TOP_K):
            ff = ff + _load_row_tiles(buf, R, lead=(s, k)) * gate[:, k:k + 1]
        is_prompt = i < n_prompt_tiles

        @pl.when(is_prompt)
        def _():
            yp_ref[...] = _rms(x1_ref[...] + g2p_ref[...] * ff, nf_ref[...])

        @pl.when(jnp.logical_not(is_prompt))
        def _():
            ysm_ref[...] = _rms(x1_ref[...] + g2s_ref[...] * ff, nf_ref[...])

    for s in range(2):
        pl.when(slot == s)(functools.partial(step, s))


def _combine(dest, ys, x1, gates, mod_p, mod_s, norm_final, n_prompt, rows_per_batch):
    n, d = x1.shape
    R = ROW_TILE
    dest, rec = _dest_records(dest, R)
    npt = n_prompt // R
    tiles = rows_per_batch // R
    nbatch = n_prompt // rows_per_batch
    n_s = n - n_prompt
    return pl.pallas_call(
        functools.partial(_combine_kernel, n_prompt_tiles=npt),
        out_shape=(jax.ShapeDtypeStruct((n_prompt, d), F32), jax.ShapeDtypeStruct((n_s, d), F32)),
        grid=(n // R,),
        in_specs=[pl.BlockSpec(memory_space=pl.ANY),
                  pl.BlockSpec(memory_space=pl.ANY),
                  pl.BlockSpec((R, d), lambda i: (i, 0)),
                  pl.BlockSpec((R, LANES), lambda i: (i, 0)),
                  pl.BlockSpec((None, 1, d), lambda i: (jnp.minimum(i // tiles, nbatch - 1), 0, 5)),
                  pl.BlockSpec((R, d), lambda i: (jnp.maximum(i - npt, 0), 5)),
                  pl.BlockSpec((1, d), lambda i: (0, 0))],
        out_specs=(pl.BlockSpec((R, d), lambda i: (jnp.minimum(i, npt - 1), 0)),
                   pl.BlockSpec((R, d), lambda i: (jnp.maximum(i - npt, 0), 0))),
        scratch_shapes=[pltpu.SMEM((2 * rec,), I32), pltpu.VMEM((2, TOP_K, R * SUBLANES, LANES), F32),
                        pltpu.SemaphoreType.DMA((2,)), pltpu.SemaphoreType.DMA((2,))],
        compiler_params=_cparams(("arbitrary",)),
        name="combine",
    )(dest, ys, x1, gates, mod_p, mod_s, norm_final)


def _reorder_w_in(w):
    W = GROUP_W
    g0 = 4 * W
    g1 = g0 + 2 * N_HEADS
    pad = jnp.zeros((w.shape[0], LANES - 2 * N_HEADS), w.dtype)
    cols = [w[:, 0:g0], w[:, g1:g1 + W], w[:, g1 + 2 * W:g1 + 4 * W], w[:, g1 + W:g1 + 2 * W], w[:, g0:g1], pad]
    return jnp.concatenate(cols, axis=1).astype(BF16)


def kernel(x_prompt, x_sample, c_prompt, c_sample, state_mlstm_C, state_mlstm_n, state_mlstm_m,
           state_hgrn_S, w_ada, b_ada, norm_mix, norm_ffn, w_in, b_gate, norm_a, lb_logits, norm_b,
           w_out, router_w, router_b, w_gu, b_gu, w_down, b_down, norm_final):
    bp, seq, d = x_prompt.shape
    bs = x_sample.shape[0]
    assert x_sample.shape[1] == 1 and w_ada.shape[0] == 1 and d == SUBLANES * LANES
    n_p = bp * seq
    n_all = n_p + bs
    n_exp = router_w.shape[2]
    W = GROUP_W
    n_a = 7 * W

    mod = _ada(jnp.concatenate([c_prompt, c_sample], axis=0), w_ada[0], b_ada)
    mod_p = mod[:bp].reshape(bp, 1, 6 * d)
    mod_s = mod[bp:]

    w_r = _reorder_w_in(w_in[0])
    nmix = norm_mix.reshape(1, d)
    tm = min(TOKEN_TILE, seq)
    xp = x_prompt.reshape(n_p, d)
    xs_ = x_sample.reshape(bs, d)
    pa_p, pb_p = _inproj(xp, mod_p, False, nmix, w_r, n_a, tm, seq, BF16)
    pa_s, pb_s = _inproj(xs_, mod_s, True, nmix, w_r, n_a, bs, None, F32)

    bg_row = jnp.pad(b_gate.reshape(1, 2 * N_HEADS), ((0, 0), (0, LANES - 2 * N_HEADS)))
    na = norm_a.reshape(1, W)
    nb_ = norm_b.reshape(1, W)
    ha_p, c_p, nrm_p, m_p = _mlstm_prompt(pa_p, pb_p, bg_row, na, bp, seq)
    hb_p, s_p = _hgrn_prompt(pa_p, pb_p, lb_logits, nb_, bp, seq)
    ha_s, hb_s, c_s, nrm_s, m_s, s_s = _step_mixers(
        pa_s, pb_s, bg_row, na, nb_, lb_logits, state_mlstm_C, state_mlstm_n, state_mlstm_m, state_hgrn_S)

    w_o = w_out[0].astype(BF16)
    rw = jnp.pad(router_w[0], ((0, 0), (0, LANES - n_exp))).astype(BF16)
    rb = jnp.pad(router_b.reshape(1, n_exp), ((0, 0), (0, LANES - n_exp)), constant_values=NEG)
    nffn = norm_ffn.reshape(1, d)
    x1, h2, idx, gates, rank, counts = _post(ha_p, hb_p, xp, mod_p, ha_s, hb_s, xs_, mod_s, nffn, w_o, rw, rb,
                                             tm, seq)

    bm = EXPERT_BLOCK
    cnt = counts[0].astype(I32)
    padded = ((cnt + bm - 1) // bm) * bm
    ends = jnp.cumsum(padded)
    start_row = (ends - padded).astype(F32).reshape(1, LANES)
    dest = _dest(idx, rank, start_row)
    n_blocks = (n_all * TOP_K + n_exp * (bm - 1) + bm - 1) // bm
    n_valid = (ends[n_exp - 1] // bm).astype(I32)
    blk_start = jnp.arange(n_blocks, dtype=I32) * bm
    block_e = jnp.sum((ends[None, :n_exp] <= blk_start[:, None]).astype(I32), axis=1)
    block_e = jnp.minimum(block_e, n_exp - 1)
    last_e = block_e[jnp.maximum(n_valid - 1, 0)]
    block_e = jnp.where(jnp.arange(n_blocks) < n_valid, block_e, last_e)

    min_valid = -(-(n_all * TOP_K) // bm)
    group_last = jnp.where(padded[:n_exp] > 0, ends[:n_exp] // bm - 1, -1)
    tail = n_valid + jnp.arange(n_blocks - min_valid, dtype=I32)
    tail = jnp.where(tail < n_blocks, tail, -1)
    zero_blocks = jnp.concatenate([group_last.astype(I32), tail])

    xs_sorted = _dispatch(zero_blocks, dest, h2, n_blocks * bm)
    ys_sorted = _experts(block_e, n_valid.reshape(1), xs_sorted, w_gu, b_gu[0], w_down, b_down[0])
    y_p, y_s = _combine(dest, ys_sorted, x1, gates, mod_p, mod_s, norm_final.reshape(1, d), n_p, seq)

    m_p_out = m_p[:, :N_HEADS, 0][None]
    return (y_p.reshape(bp, seq, d), y_s.reshape(bs, 1, d), c_p, nrm_p, m_p_out, s_p,
            c_s, nrm_s, m_s, s_s)
```
